```python
import math
import jax
import jax.numpy as jnp
from jax import lax
import numpy as np

D_MODEL = 1024
BATCH = 4
SEQ = 4096
DEPTH = 2

EPS = 1e-6
CONV_K = 4

SSD_WIDTH = D_MODEL
SSD_HEAD_DIM = 64
SSD_HEADS = SSD_WIDTH // SSD_HEAD_DIM
SSD_GROUPS = 2
SSD_STATE = 128
SSD_CHUNK = 128
SSD_CONV_DIM = SSD_WIDTH + 2 * SSD_GROUPS * SSD_STATE

ML_WIDTH = D_MODEL // 2
ML_HEADS = 4
ML_HEAD_DIM = ML_WIDTH // ML_HEADS
ML_CHUNK = 128

HG_WIDTH = D_MODEL // 2
HG_HEADS = 4
HG_HEAD_DIM = HG_WIDTH // HG_HEADS
HG_CHUNK = 16

IN_SPLITS = (SSD_WIDTH, SSD_GROUPS * SSD_STATE, SSD_GROUPS * SSD_STATE, SSD_HEADS, SSD_WIDTH,
             ML_WIDTH, ML_WIDTH, ML_WIDTH,
             HG_WIDTH, HG_WIDTH, HG_WIDTH, HG_WIDTH,
             D_MODEL, D_MODEL, D_MODEL)
N_IN = sum(IN_SPLITS)

kernel_name = 'hybrid_ssd_mlstm_hgrn2'


def rmsnorm(x, w):
    xf = x.astype(jnp.float32)
    y = xf * lax.rsqrt(jnp.mean(xf * xf, -1, keepdims=True) + EPS)
    return (y * w).astype(x.dtype)


def grouped_rmsnorm(x, w, groups):
    sh = x.shape
    xf = x.astype(jnp.float32).reshape(*sh[:-1], groups, sh[-1] // groups)
    y = xf * lax.rsqrt(jnp.mean(xf * xf, -1, keepdims=True) + EPS)
    return y.reshape(sh) * w


def head_layernorm(x, w):
    xf = x.astype(jnp.float32)
    mu = jnp.mean(xf, -1, keepdims=True)
    xc = xf - mu
    y = xc * lax.rsqrt(jnp.mean(xc * xc, -1, keepdims=True) + EPS)
    b, t, h, d = x.shape
    return y.reshape(b, t, h * d) * w


def causal_dwconv(x, w, bias):
    k = w.shape[0]
    y = lax.conv_general_dilated(x, w[:, None, :], window_strides=(1,), padding=[(k - 1, 0)],
                                 dimension_numbers=('NWC', 'WIO', 'NWC'),
                                 feature_group_count=x.shape[-1])
    return y + bias


def causal_log_decay(a):
    cs = jnp.cumsum(a, -1)
    n = a.shape[-1]
    mask = jnp.tril(jnp.ones((n, n), dtype=bool))
    return jnp.where(mask, cs[..., :, None] - cs[..., None, :], -jnp.inf)


def scan_chunk_states(decay, local):
    def step(s, inp):
        d, l = inp
        return d * s + l, s
    _, s_in = lax.scan(step, jnp.zeros_like(local[0]), (decay, local))
    return s_in


def ssd_scan(xs, bm, cm, dt_raw, dt_bias, a_log, d_skip):
    f32 = jnp.float32
    b, t, h, p = xs.shape
    g, n = bm.shape[2], bm.shape[3]
    hg = h // g
    nc = t // SSD_CHUNK
    dt = jax.nn.softplus(dt_raw.astype(f32) + dt_bias.astype(f32))
    a = -jnp.exp(a_log.astype(f32))
    la = (dt * a).reshape(b, nc, SSD_CHUNK, g, hg).transpose(0, 3, 4, 1, 2)
    cum = jnp.cumsum(la, -1)
    x_dt = (xs.astype(f32) * dt[..., None]).reshape(b, nc, SSD_CHUNK, g, hg, p)
    bc = bm.astype(f32).reshape(b, nc, SSD_CHUNK, g, n)
    cc = cm.astype(f32).reshape(b, nc, SSD_CHUNK, g, n)
    cb = jnp.einsum('bclgn,bcsgn->bgcls', cc, bc)
    scores = cb[:, :, None] * jnp.exp(causal_log_decay(la))
    y_diag = jnp.einsum('bghcls,bcsghp->bclghp', scores, x_dt)
    w_end = jnp.exp(cum[..., -1:] - cum)
    local = jnp.einsum('bcsgn,bghcs,bcsghp->cbghpn', bc, w_end, x_dt)
    chunk_decay = jnp.exp(cum[..., -1]).transpose(3, 0, 1, 2)[..., None, None]
    s_in = scan_chunk_states(chunk_decay, local)
    y_off = jnp.einsum('bclgn,cbghpn,bghcl->bclghp', cc, s_in, jnp.exp(cum))
    y = (y_diag + y_off).reshape(b, t, h, p)
    return y + xs.astype(f32) * d_skip.astype(f32)[:, None]


def mlstm_scan(q, k, v, i_pre, f_pre):
    f32 = jnp.float32
    b, t, h, d = q.shape
    L = ML_CHUNK
    nc = t // L
    qc = (q.astype(f32) * (d ** -0.5)).reshape(b, nc, L, h, d)
    kc = k.astype(f32).reshape(b, nc, L, h, d)
    vc = v.astype(f32).reshape(b, nc, L, h, d)
    li = i_pre.astype(f32).reshape(b, nc, L, h).transpose(0, 3, 1, 2)
    lf = jax.nn.log_sigmoid(f_pre.astype(f32)).reshape(b, nc, L, h).transpose(0, 3, 1, 2)
    cum = jnp.cumsum(lf, -1)
    log_d = causal_log_decay(lf) + li[..., None, :]
    log_end = cum[..., -1:] - cum + li
    m_loc = jnp.max(log_end, -1)
    w_end = jnp.exp(log_end - m_loc[..., None])
    c_loc = jnp.einsum('bhcs,bcshd,bcshe->cbhde', w_end, vc, kc)
    n_loc = jnp.einsum('bhcs,bcshe->cbhe', w_end, kc)
    g_end = cum[..., -1].transpose(2, 0, 1)
    m_loc_c = m_loc.transpose(2, 0, 1)

    def step(carry, inp):
        c_s, n_s, m_s = carry
        g, ml, cl, nl = inp
        m_new = jnp.maximum(g + m_s, ml)
        a_old = jnp.exp(g + m_s - m_new)
        a_loc = jnp.exp(ml - m_new)
        c_new = a_old[..., None, None] * c_s + a_loc[..., None, None] * cl
        n_new = a_old[..., None] * n_s + a_loc[..., None] * nl
        return (c_new, n_new, m_new), (c_s, n_s, m_s)

    init = (jnp.zeros((b, h, d, d), f32), jnp.zeros((b, h, d), f32), jnp.zeros((b, h), f32))
    _, (c_in, n_in, m_in) = lax.scan(step, init, (g_end, m_loc_c, c_loc, n_loc))
    m_in = m_in.transpose(1, 2, 0)
    log_inter = cum + m_in[..., None]
    m_t = jnp.maximum(log_inter, jnp.max(log_d, -1))
    w_intra = jnp.exp(log_d - m_t[..., None])
    w_inter = jnp.exp(log_inter - m_t)
    s = jnp.einsum('bclhe,bcshe->bhcls', qc, kc) * w_intra
    num = (jnp.einsum('bhcls,bcshd->bclhd', s, vc)
           + jnp.einsum('bclhe,cbhde,bhcl->bclhd', qc, c_in, w_inter))
    den = jnp.sum(s, -1) + jnp.einsum('bclhe,cbhe,bhcl->bhcl', qc, n_in, w_inter)
    den = jnp.maximum(jnp.abs(den), jnp.exp(-m_t))
    out = num / den.transpose(0, 2, 3, 1)[..., None]
    return out.reshape(b, t, h, d)


def hgrn2_scan(q, k, v, log_f):
    f32 = jnp.float32
    b, t, h, dk = q.shape
    dv = v.shape[-1]
    L = HG_CHUNK
    nc = t // L
    qc = q.astype(f32).reshape(b, nc, L, h, dk)
    kc = k.astype(f32).reshape(b, nc, L, h, dk)
    vc = v.astype(f32).reshape(b, nc, L, h, dv)
    cum = jnp.cumsum(log_f.astype(f32).reshape(b, nc, L, h, dk), axis=2)
    mask = jnp.tril(jnp.ones((L, L), dtype=bool))[None, None, :, :, None, None]
    diff = cum[:, :, :, None] - cum[:, :, None, :]
    dec = jnp.exp(jnp.where(mask, diff, -jnp.inf))
    attn = jnp.einsum('bclhk,bcshk,bclshk->bchls', qc, kc, dec)
    y_intra = jnp.einsum('bchls,bcshv->bclhv', attn, vc)
    w_end = jnp.exp(cum[:, :, -1:] - cum)
    local = jnp.einsum('bcshk,bcshv->cbhkv', kc * w_end, vc)
    chunk_decay = jnp.exp(cum[:, :, -1]).transpose(1, 0, 2, 3)[..., None]
    s_in = scan_chunk_states(chunk_decay, local)
    y_inter = jnp.einsum('bclhk,cbhkv->bclhv', qc * jnp.exp(cum), s_in)
    return (y_intra + y_inter).reshape(b, t, h, dv)


def hybrid_layer(x, lb, norm_w, w_in, ssd_conv_w, ssd_conv_b, ssd_dt_bias, ssd_a_log, ssd_d,
                 ssd_norm_w, ml_conv_w, ml_conv_b, ml_wq, ml_wk, ml_wv, ml_w_if, ml_b_if,
                 ml_norm_w, ml_skip, hg_norm_w, w_branch_ssd, w_branch_ml, w_branch_hg, w_out):
    f32 = jnp.float32
    b, t, _ = x.shape
    hn = rmsnorm(x, norm_w)
    proj = hn @ w_in
    split_points = [int(v) for v in np.cumsum(IN_SPLITS)[:-1]]
    (s_x, s_b, s_c, s_dt, s_z, m_x, m_o, m_z, g_q, g_f, g_i, g_z,
     gate_ssd, gate_ml, gate_hg) = jnp.split(proj, split_points, axis=-1)

    xbc = jax.nn.silu(causal_dwconv(jnp.concatenate([s_x, s_b, s_c], -1), ssd_conv_w, ssd_conv_b))
    sx, sb, sc = jnp.split(xbc, [SSD_WIDTH, SSD_WIDTH + SSD_GROUPS * SSD_STATE], axis=-1)
    y_ssd = ssd_scan(sx.reshape(b, t, SSD_HEADS, SSD_HEAD_DIM),
                     sb.reshape(b, t, SSD_GROUPS, SSD_STATE),
                     sc.reshape(b, t, SSD_GROUPS, SSD_STATE),
                     s_dt, ssd_dt_bias, ssd_a_log, ssd_d)
    y_ssd = grouped_rmsnorm(y_ssd.reshape(b, t, SSD_WIDTH) * jax.nn.silu(s_z.astype(f32)),
                            ssd_norm_w, SSD_GROUPS)

    m_conv = jax.nn.silu(causal_dwconv(m_x, ml_conv_w, ml_conv_b))
    xc_h = m_conv.reshape(b, t, ML_HEADS, ML_HEAD_DIM)
    q = jnp.einsum('bthd,hde->bthe', xc_h, ml_wq)
    k = jnp.einsum('bthd,hde->bthe', xc_h, ml_wk)
    v = jnp.einsum('bthd,hde->bthe', m_x.reshape(b, t, ML_HEADS, ML_HEAD_DIM), ml_wv)
    qkv = jnp.concatenate([q, k, v], -1).reshape(b, t, 3 * ML_WIDTH)
    if_pre = qkv @ ml_w_if + ml_b_if
    i_pre, f_pre = jnp.split(if_pre, 2, axis=-1)
    h_ml = mlstm_scan(q, k, v, i_pre, f_pre)
    y_ml = (head_layernorm(h_ml, ml_norm_w) * jax.nn.sigmoid(m_o.astype(f32))
            + ml_skip * m_conv.astype(f32))
    y_ml = y_ml * jax.nn.silu(m_z.astype(f32))

    fx = g_f.astype(f32)
    log_f = jnp.logaddexp(jax.nn.log_sigmoid(fx), jnp.log(lb) + jax.nn.log_sigmoid(-fx))
    k_hg = (1.0 - lb) * jax.nn.sigmoid(-fx)
    q_hg = jax.nn.silu(g_q.astype(f32))
    o_hg = hgrn2_scan(q_hg.reshape(b, t, HG_HEADS, HG_HEAD_DIM),
                      k_hg.reshape(b, t, HG_HEADS, HG_HEAD_DIM),
                      g_i.reshape(b, t, HG_HEADS, HG_HEAD_DIM),
                      log_f.reshape(b, t, HG_HEADS, HG_HEAD_DIM))
    y_hg = grouped_rmsnorm(o_hg.reshape(b, t, HG_WIDTH), hg_norm_w, HG_HEADS) * jax.nn.silu(g_z.astype(f32))

    dt_ = x.dtype
    merged = (jax.nn.sigmoid(gate_ssd) * (y_ssd.astype(dt_) @ w_branch_ssd)
              + jax.nn.sigmoid(gate_ml) * (y_ml.astype(dt_) @ w_branch_ml)
              + jax.nn.sigmoid(gate_hg) * (y_hg.astype(dt_) @ w_branch_hg))
    return x + (merged @ w_out).astype(dt_)


def setup_inputs(seed: int = 0) -> dict:
    key = jax.random.key(seed)
    ks = jax.random.split(key, 26)
    f32 = jnp.float32

    def nrm(k, shape, scale):
        return jax.random.normal(k, shape, f32) * scale

    def gain(k, shape):
        return 1.0 + 0.02 * jax.random.normal(k, shape, f32)

    x = nrm(ks[0], (BATCH, SEQ, D_MODEL), 1.0)
    norm_w = gain(ks[1], (DEPTH, D_MODEL))
    w_in = nrm(ks[2], (DEPTH, D_MODEL, N_IN), D_MODEL ** -0.5)
    ssd_conv_w = nrm(ks[3], (DEPTH, CONV_K, SSD_CONV_DIM), CONV_K ** -0.5)
    ssd_conv_b = nrm(ks[4], (DEPTH, SSD_CONV_DIM), 0.02)
    dt0 = jnp.exp(jax.random.uniform(ks[5], (DEPTH, SSD_HEADS), f32, math.log(1e-3), math.log(1e-1)))
    ssd_dt_bias = dt0 + jnp.log(-jnp.expm1(-dt0))
    ssd_a_log = jnp.log(jax.random.uniform(ks[6], (DEPTH, SSD_HEADS), f32, 1.0, 16.0))
    ssd_d = gain(ks[7], (DEPTH, SSD_HEADS))
    ssd_norm_w = gain(ks[8], (DEPTH, SSD_WIDTH))
    ml_conv_w = nrm(ks[9], (DEPTH, CONV_K, ML_WIDTH), CONV_K ** -0.5)
    ml_conv_b = nrm(ks[10], (DEPTH, ML_WIDTH), 0.02)
    ml_wq = nrm(ks[11], (DEPTH, ML_HEADS, ML_HEAD_DIM, ML_HEAD_DIM), ML_HEAD_DIM ** -0.5)
    ml_wk = nrm(ks[12], (DEPTH, ML_HEADS, ML_HEAD_DIM, ML_HEAD_DIM), ML_HEAD_DIM ** -0.5)
    ml_wv = nrm(ks[13], (DEPTH, ML_HEADS, ML_HEAD_DIM, ML_HEAD_DIM), ML_HEAD_DIM ** -0.5)
    ml_w_if = nrm(ks[14], (DEPTH, 3 * ML_WIDTH, 2 * ML_HEADS), (3 * ML_WIDTH) ** -0.5)
    i_bias = nrm(ks[15], (DEPTH, ML_HEADS), 0.1)
    f_bias = jnp.linspace(3.0, 6.0, ML_HEADS, dtype=f32)[None] + nrm(ks[16], (DEPTH, ML_HEADS), 0.02)
    ml_b_if = jnp.concatenate([i_bias, f_bias], -1)
    ml_norm_w = gain(ks[17], (DEPTH, ML_WIDTH))
    ml_skip = gain(ks[18], (DEPTH, ML_WIDTH))
    hg_lower_bounds = nrm(ks[19], (DEPTH, HG_WIDTH), 0.1)
    hg_norm_w = gain(ks[20], (DEPTH, HG_WIDTH))
    w_branch_ssd = nrm(ks[21], (DEPTH, SSD_WIDTH, D_MODEL), SSD_WIDTH ** -0.5)
    w_branch_ml = nrm(ks[22], (DEPTH, ML_WIDTH, D_MODEL), ML_WIDTH ** -0.5)
    w_branch_hg = nrm(ks[23], (DEPTH, HG_WIDTH, D_MODEL), HG_WIDTH ** -0.5)
    w_out = nrm(ks[24], (DEPTH, D_MODEL, D_MODEL), D_MODEL ** -0.5)
    final_norm_w = gain(ks[25], (D_MODEL,))
    return {'x': x, 'norm_w': norm_w, 'w_in': w_in, 'ssd_conv_w': ssd_conv_w,
            'ssd_conv_b': ssd_conv_b, 'ssd_dt_bias': ssd_dt_bias, 'ssd_a_log': ssd_a_log,
            'ssd_d': ssd_d, 'ssd_norm_w': ssd_norm_w, 'ml_conv_w': ml_conv_w,
            'ml_conv_b': ml_conv_b, 'ml_wq': ml_wq, 'ml_wk': ml_wk, 'ml_wv': ml_wv,
            'ml_w_if': ml_w_if, 'ml_b_if': ml_b_if, 'ml_norm_w': ml_norm_w, 'ml_skip': ml_skip,
            'hg_lower_bounds': hg_lower_bounds, 'hg_norm_w': hg_norm_w,
            'w_branch_ssd': w_branch_ssd, 'w_branch_ml': w_branch_ml, 'w_branch_hg': w_branch_hg,
            'w_out': w_out, 'final_norm_w': final_norm_w}


def reference(x, norm_w, w_in, ssd_conv_w, ssd_conv_b, ssd_dt_bias, ssd_a_log, ssd_d, ssd_norm_w,
              ml_conv_w, ml_conv_b, ml_wq, ml_wk, ml_wv, ml_w_if, ml_b_if, ml_norm_w, ml_skip,
              hg_lower_bounds, hg_norm_w, w_branch_ssd, w_branch_ml, w_branch_hg, w_out,
              final_norm_w):
    lbs = jnp.cumsum(jax.nn.softmax(hg_lower_bounds.astype(jnp.float32), axis=0), axis=0)
    lbs = lbs - lbs[0]
    for l in range(DEPTH):
        x = hybrid_layer(x, lbs[l], norm_w[l], w_in[l], ssd_conv_w[l], ssd_conv_b[l],
                         ssd_dt_bias[l], ssd_a_log[l], ssd_d[l], ssd_norm_w[l],
                         ml_conv_w[l], ml_conv_b[l], ml_wq[l], ml_wk[l], ml_wv[l],
                         ml_w_if[l], ml_b_if[l], ml_norm_w[l], ml_skip[l], hg_norm_w[l],
                         w_branch_ssd[l], w_branch_ml[l], w_branch_hg[l], w_out[l])
    return rmsnorm(x, final_norm_w)
```

```python
import functools

import numpy as np
import jax
import jax.numpy as jnp
from jax import lax
from jax.experimental import pallas as pl
from jax.experimental.pallas import tpu as pltpu

F32 = jnp.float32
BF16 = jnp.bfloat16

EPS = 1e-6
CONV_K = 4
LANES = 128
SUBLANES = 8
CHUNK = 128
TOKEN_BLOCK = 512
VMEM_LIMIT_BYTES = 56 * 1024 * 1024

SSD_HEAD_DIM = 64
SSD_HEADS = 16
SSD_GROUPS = 2
SSD_STATE = 128
SSD_WIDTH = SSD_HEADS * SSD_HEAD_DIM
SSD_CONV_DIM = SSD_WIDTH + 2 * SSD_GROUPS * SSD_STATE
ML_HEADS = 4
ML_HEAD_DIM = 128
ML_WIDTH = ML_HEADS * ML_HEAD_DIM
HG_HEADS = 4
HG_HEAD_DIM = 128
HG_WIDTH = HG_HEADS * HG_HEAD_DIM
HG_LEVELS = 7


def _dot(a, b):
    return jnp.dot(a, b, preferred_element_type=F32)


def _dot_nt(a, b):
    return lax.dot_general(a, b, (((1,), (1,)), ((), ())), preferred_element_type=F32)


def _rmsnorm(x, w):
    return x * lax.rsqrt(jnp.mean(x * x, axis=-1, keepdims=True) + EPS) * w


def _softplus(x):
    return jnp.maximum(x, 0.0) + jnp.log1p(jnp.exp(-jnp.abs(x)))


def _log_sigmoid(x):
    return jnp.minimum(x, 0.0) - jnp.log1p(jnp.exp(-jnp.abs(x)))


def _silu(x):
    return x * jax.nn.sigmoid(x)


def _tril_ones_bf16(n):
    r = lax.broadcasted_iota(jnp.int32, (n, n), 0)
    c = lax.broadcasted_iota(jnp.int32, (n, n), 1)
    return jnp.where(c <= r, 1.0, 0.0).astype(BF16)


def _cumsum_time(x, tri):
    hi = x.astype(BF16)
    r1 = x - hi.astype(F32)
    mid = r1.astype(BF16)
    lo = (r1 - mid.astype(F32)).astype(BF16)
    return _dot(tri, hi) + _dot(tri, mid) + _dot(tri, lo)


def _colb(x, j, n=LANES):
    return jnp.broadcast_to(x[:, j:j + 1], (x.shape[0], n))


def _rowb(x, j, m):
    return jnp.broadcast_to(x[j:j + 1, :], (m, x.shape[1]))


def _expand_heads(v):
    rows = v.shape[0]
    lane = lax.broadcasted_iota(jnp.int32, (rows, LANES), 1)
    parts = []
    for j in range(SSD_HEADS // 2):
        a = _colb(v, 2 * j)
        b = _colb(v, 2 * j + 1)
        parts.append(jnp.where(lane < SSD_HEAD_DIM, a, b))
    return jnp.concatenate(parts, axis=1)


def _causal_conv_silu(pre_ref, cw_ref, cb_ref, out_ref, tb):
    acc = cb_ref[...] + cw_ref[0:1, :] * pre_ref[pl.ds(SUBLANES - 3, tb), :]
    for j in range(1, CONV_K):
        acc = acc + cw_ref[j:j + 1, :] * pre_ref[pl.ds(SUBLANES - 3 + j, tb), :]
    out_ref[...] = _silu(acc)
    pre_ref[0:SUBLANES, :] = pre_ref[pl.ds(tb, SUBLANES), :]


def _ssd_kernel(x_ref, nw_ref, w_ref, cw_ref, cb_ref, dtb_ref, alog_ref, dskip_ref, gnw_ref,
                y_ref, pre_ref, xc_ref, dt_ref, z_ref, st_ref):
    tb = x_ref.shape[0]
    L = CHUNK

    @pl.when(pl.program_id(1) == 0)
    def _():
        pre_ref[0:SUBLANES, :] = jnp.zeros((SUBLANES, SSD_CONV_DIM), F32)
        st_ref[...] = jnp.zeros(st_ref.shape, F32)

    hn = _rmsnorm(x_ref[...], nw_ref[...]).astype(BF16)
    pre_ref[pl.ds(SUBLANES, tb), :] = _dot(hn, w_ref[:, 0:SSD_CONV_DIM])
    dt_ref[...] = _dot(hn, w_ref[:, SSD_CONV_DIM:SSD_CONV_DIM + LANES])
    z_ref[...] = _dot(hn, w_ref[:, SSD_CONV_DIM + LANES:])
    _causal_conv_silu(pre_ref, cw_ref, cb_ref, xc_ref, tb)

    lane_row = lax.broadcasted_iota(jnp.int32, (1, LANES), 1)
    a_row = jnp.where(lane_row < SSD_HEADS, -jnp.exp(alog_ref[...]), 0.0)
    r_i = lax.broadcasted_iota(jnp.int32, (L, L), 0)
    c_i = lax.broadcasted_iota(jnp.int32, (L, L), 1)
    causal = c_i <= r_i
    lane = lax.broadcasted_iota(jnp.int32, (L, LANES), 1)
    lo_half = lane < SSD_HEAD_DIM
    tri = _tril_ones_bf16(L)
    gs = SSD_WIDTH // SSD_GROUPS
    hpg = SSD_HEADS // SSD_GROUPS

    def chunk(c, carry):
        r0 = pl.multiple_of(c * L, L)
        rows = pl.ds(r0, L)
        xs = xc_ref[rows, 0:SSD_WIDTH]
        dt = _softplus(dt_ref[rows, :] + dtb_ref[...])
        cum = _cumsum_time(dt * a_row, tri)
        cum_t = cum.T
        dt_t = dt.T
        cum_last = cum[L - 1:L, :]
        w_end = jnp.exp(cum_last - cum)
        dec_x = _expand_heads(jnp.exp(cum))
        dtw_x = _expand_heads(dt * w_end)
        dec_last_x = _expand_heads(jnp.exp(cum_last))
        xw = (xs * dtw_x).astype(BF16)
        xs_b = xs.astype(BF16)

        y_parts = []
        for g in range(SSD_GROUPS):
            bm = xc_ref[rows, SSD_WIDTH + g * SSD_STATE:SSD_WIDTH + (g + 1) * SSD_STATE]
            cm = xc_ref[rows, SSD_WIDTH + (SSD_GROUPS + g) * SSD_STATE:
                        SSD_WIDTH + (SSD_GROUPS + g + 1) * SSD_STATE]
            cm_b = cm.astype(BF16)
            cb = _dot_nt(cm_b, bm.astype(BF16))
            h_t = st_ref[g]
            y_off = _dot(cm_b, h_t.astype(BF16))
            st_ref[g] = (h_t * dec_last_x[:, g * gs:(g + 1) * gs]
                         + _dot(bm.T.astype(BF16), xw[:, g * gs:(g + 1) * gs]))
            for jp in range(hpg // 2):
                pair = g * (hpg // 2) + jp
                sc = []
                for h in (2 * pair, 2 * pair + 1):
                    seg = _colb(cum, h) - _rowb(cum_t, h, L)
                    lm = jnp.exp(jnp.where(causal, seg, -jnp.inf))
                    sc.append((cb * lm * _rowb(dt_t, h, L)).astype(BF16))
                xp = xs_b[:, pair * LANES:(pair + 1) * LANES]
                zero = jnp.zeros_like(xp)
                rhs = jnp.concatenate([jnp.where(lo_half, xp, zero), jnp.where(lo_half, zero, xp)], axis=0)
                y_diag = _dot(jnp.concatenate(sc, axis=1), rhs)
                y_parts.append(y_diag + y_off[:, jp * LANES:(jp + 1) * LANES]
                               * dec_x[:, pair * LANES:(pair + 1) * LANES])
        y = jnp.concatenate(y_parts, axis=1) + xs * dskip_ref[...]
        y = y * _silu(z_ref[rows, :])
        outs = []
        for g in range(SSD_GROUPS):
            yg = y[:, g * gs:(g + 1) * gs]
            outs.append(yg * lax.rsqrt(jnp.mean(yg * yg, axis=-1, keepdims=True) + EPS))
        y_ref[rows, :] = (jnp.concatenate(outs, axis=1) * gnw_ref[...]).astype(BF16)
        return carry

    lax.fori_loop(0, tb // L, chunk, 0)


def _ml_kernel(x_ref, nw_ref, w_ref, cw_ref, cb_ref, wqk_ref, wv_ref, wif_ref, bif_ref, lnw_ref,
               skip_ref, y_ref, pre_ref, mc_ref, oz_ref, ct_ref, n_ref, m_ref):
    tb = x_ref.shape[0]
    L = CHUNK
    D = ML_HEAD_DIM

    @pl.when(pl.program_id(1) == 0)
    def _():
        pre_ref[0:SUBLANES, :] = jnp.zeros((SUBLANES, ML_WIDTH), F32)
        ct_ref[...] = jnp.zeros(ct_ref.shape, F32)
        n_ref[...] = jnp.zeros(n_ref.shape, F32)
        m_ref[...] = jnp.zeros(m_ref.shape, F32)

    hn = _rmsnorm(x_ref[...], nw_ref[...]).astype(BF16)
    pre_ref[pl.ds(SUBLANES, tb), :] = _dot(hn, w_ref[:, 0:ML_WIDTH])
    oz_ref[...] = _dot(hn, w_ref[:, ML_WIDTH:])
    mx_rows0 = SUBLANES
    _causal_conv_silu(pre_ref, cw_ref, cb_ref, mc_ref, tb)

    r_i = lax.broadcasted_iota(jnp.int32, (L, L), 0)
    c_i = lax.broadcasted_iota(jnp.int32, (L, L), 1)
    causal = c_i <= r_i
    tri = _tril_ones_bf16(L)
    scale = D ** -0.5

    def chunk(c, carry):
        r0 = pl.multiple_of(c * L, L)
        rows = pl.ds(r0, L)
        mconv = mc_ref[rows, :]
        mx = pre_ref[pl.ds(r0 + mx_rows0, L), :]
        qs, ks, vs = [], [], []
        if_pre = jnp.broadcast_to(bif_ref[...], (L, LANES))
        for h in range(ML_HEADS):
            cols = slice(h * D, (h + 1) * D)
            qk = _dot(mconv[:, cols].astype(BF16), wqk_ref[h])
            v = _dot(mx[:, cols].astype(BF16), wv_ref[h])
            q, k = qk[:, 0:D], qk[:, D:2 * D]
            qkv = jnp.concatenate([q, k, v], axis=1).astype(BF16)
            if_pre = if_pre + _dot(qkv, wif_ref[h])
            qs.append(q)
            ks.append(k)
            vs.append(v)
        lf = _log_sigmoid(if_pre)
        cum = _cumsum_time(lf, tri)
        cum_t = cum.T
        if_t = if_pre.T
        for h in range(ML_HEADS):
            cols = slice(h * D, (h + 1) * D)
            q, k, v = qs[h], ks[h], vs[h]
            q_b = (q * scale).astype(BF16)
            k_b = k.astype(BF16)
            c_col = _colb(cum, ML_HEADS + h)
            c_row = _rowb(cum_t, ML_HEADS + h, L)
            i_row = _rowb(if_t, h, L)
            i_col = _colb(if_pre, h)
            m_in = m_ref[h:h + 1, :]
            n_in = n_ref[h:h + 1, :]
            ct_in = ct_ref[h]
            log_d = jnp.where(causal, c_col - c_row + i_row, -jnp.inf)
            log_inter = c_col + m_in
            m_t = jnp.maximum(log_inter, jnp.max(log_d, axis=1, keepdims=True))
            w_intra = jnp.exp(log_d - m_t)
            w_inter = jnp.exp(log_inter - m_t)
            s = _dot_nt(q_b, k_b) * w_intra
            num = _dot(s.astype(BF16), v.astype(BF16)) + w_inter * _dot(q_b, ct_in.astype(BF16))
            den = (jnp.sum(s, axis=1, keepdims=True)
                   + w_inter * jnp.sum((q * scale) * n_in, axis=1, keepdims=True))
            den = jnp.maximum(jnp.abs(den), jnp.exp(-m_t))
            hh = num / den
            g_row = c_col[L - 1:L, :]
            log_end = g_row - c_col + i_col
            m_loc = jnp.max(log_end, axis=0, keepdims=True)
            w_end = jnp.exp(log_end - m_loc)
            ct_loc = _dot(k.T.astype(BF16), (v * w_end).astype(BF16))
            n_loc = jnp.sum(k * w_end, axis=0, keepdims=True)
            m_new = jnp.maximum(g_row + m_in, m_loc)
            a_old = jnp.exp(g_row + m_in - m_new)
            a_loc = jnp.exp(m_loc - m_new)
            ct_ref[h] = a_old * ct_in + a_loc * ct_loc
            n_ref[h:h + 1, :] = a_old * n_in + a_loc * n_loc
            m_ref[h:h + 1, :] = m_new
            mu = jnp.mean(hh, axis=-1, keepdims=True)
            xc = hh - mu
            ln = xc * lax.rsqrt(jnp.mean(xc * xc, axis=-1, keepdims=True) + EPS) * lnw_ref[:, cols]
            o_gate = jax.nn.sigmoid(oz_ref[rows, h * D:(h + 1) * D])
            z = oz_ref[rows, ML_WIDTH + h * D:ML_WIDTH + (h + 1) * D]
            out = (ln * o_gate + skip_ref[:, cols] * mconv[:, cols]) * _silu(z)
            y_ref[rows, cols] = out.astype(BF16)
        return carry

    lax.fori_loop(0, tb // L, chunk, 0)


def _hg_level_table():
    l = np.arange(CHUNK)[:, None]
    s = np.arange(CHUNK)[None, :]
    x = l ^ s
    msb = np.floor(np.log2(np.maximum(x, 1))).astype(np.int32)
    return np.where(s < l, msb, np.where(s == l, HG_LEVELS, HG_LEVELS + 1)).astype(np.int32)


def _hg_reference_rows(cum_ref, level, width):
    b = 1 << level
    if 2 * b >= SUBLANES:
        pieces = []
        for blk in range(CHUNK // (2 * b)):
            r = blk * 2 * b + b - 1
            pieces.append(jnp.broadcast_to(cum_ref[r:r + 1, :], (2 * b, width)))
        return jnp.concatenate(pieces, axis=0)
    sub = lax.broadcasted_iota(jnp.int32, (SUBLANES, width), 0)
    pieces = []
    for grp in range(CHUNK // SUBLANES):
        base = grp * SUBLANES
        acc = None
        for blk in range(SUBLANES // (2 * b)):
            r = base + blk * 2 * b + b - 1
            row = jnp.broadcast_to(cum_ref[r:r + 1, :], (SUBLANES, width))
            acc = row if acc is None else jnp.where(sub >= blk * 2 * b, row, acc)
        pieces.append(acc)
    return jnp.concatenate(pieces, axis=0)


def _hg_kernel(x_ref, nw_ref, w_ref, lb_ref, gnw_ref, lv_ref, y_ref, pj_ref, cum_ref, st_ref):
    tb = x_ref.shape[0]
    L = CHUNK
    D = HG_HEAD_DIM
    W = HG_WIDTH

    @pl.when(pl.program_id(1) == 0)
    def _():
        st_ref[...] = jnp.zeros(st_ref.shape, F32)

    hn = _rmsnorm(x_ref[...], nw_ref[...]).astype(BF16)
    pj_ref[...] = _dot(hn, w_ref[...])
    tri = _tril_ones_bf16(L)
    lb = lb_ref[...]
    log_lb = jnp.log(lb)

    def chunk(c, carry):
        r0 = pl.multiple_of(c * L, L)
        rows = pl.ds(r0, L)
        fx = pj_ref[rows, W:2 * W]
        ls_pos = _log_sigmoid(fx)
        b_term = log_lb + (ls_pos - fx)
        log_f = jnp.maximum(ls_pos, b_term) + jnp.log1p(jnp.exp(-jnp.abs(ls_pos - b_term)))
        k = (1.0 - lb) * jax.nn.sigmoid(-fx)
        q = _silu(pj_ref[rows, 0:W])
        v_b = pj_ref[rows, 2 * W:3 * W].astype(BF16)
        cum = _cumsum_time(log_f, tri)
        cum_ref[...] = cum
        lv = lv_ref[...]
        q_b = q.astype(BF16)
        k_b = k.astype(BF16)
        attn = []
        for h in range(HG_HEADS):
            cols = slice(h * D, (h + 1) * D)
            attn.append(jnp.where(lv == HG_LEVELS, _dot_nt(q_b[:, cols], k_b[:, cols]), 0.0))
        for level in range(HG_LEVELS):
            ref_rows = _hg_reference_rows(cum_ref, level, W)
            e = jnp.exp(-jnp.abs(cum - ref_rows))
            qe = (q * e).astype(BF16)
            ke = (k * e).astype(BF16)
            for h in range(HG_HEADS):
                cols = slice(h * D, (h + 1) * D)
                attn[h] = attn[h] + jnp.where(lv == level, _dot_nt(qe[:, cols], ke[:, cols]), 0.0)
        cum_last = cum[L - 1:L, :]
        q_dec = (q * jnp.exp(cum)).astype(BF16)
        k_end = k * jnp.exp(cum_last - cum)
        dec_last = jnp.exp(cum_last)
        gz = _silu(pj_ref[rows, 3 * W:4 * W])
        for h in range(HG_HEADS):
            cols = slice(h * D, (h + 1) * D)
            st = st_ref[h]
            o = _dot(attn[h].astype(BF16), v_b[:, cols]) + _dot_nt(q_dec[:, cols], st.astype(BF16))
            v_t = pj_ref[rows, 2 * W + h * D:2 * W + (h + 1) * D].T.astype(BF16)
            st_ref[h] = st * dec_last[:, cols] + _dot(v_t, k_end[:, cols].astype(BF16))
            on = o * lax.rsqrt(jnp.mean(o * o, axis=-1, keepdims=True) + EPS)
            y_ref[rows, cols] = (on * gnw_ref[:, cols] * gz[:, cols]).astype(BF16)
        return carry

    lax.fori_loop(0, tb // L, chunk, 0)


def _merge_kernel(x_ref, nw_ref, wg_ref, ys_ref, ym_ref, yh_ref, wbs_ref, wbm_ref, wbh_ref, wo_ref,
                  fnw_ref, o_ref, *, final_norm):
    x = x_ref[...]
    hn = _rmsnorm(x, nw_ref[...]).astype(BF16)
    d = x.shape[-1]
    merged = jax.nn.sigmoid(_dot(hn, wg_ref[:, 0:d])) * _dot(ys_ref[...], wbs_ref[...])
    merged = merged + jax.nn.sigmoid(_dot(hn, wg_ref[:, d:2 * d])) * _dot(ym_ref[...], wbm_ref[...])
    merged = merged + jax.nn.sigmoid(_dot(hn, wg_ref[:, 2 * d:3 * d])) * _dot(yh_ref[...], wbh_ref[...])
    out = x + _dot(merged.astype(BF16), wo_ref[...])
    if final_norm:
        out = _rmsnorm(out, fnw_ref[...])
    o_ref[...] = out


def _const_spec(shape):
    nd = len(shape)
    return pl.BlockSpec(shape, lambda b, t: (0,) * nd)


def _tok_spec(tb, width):
    return pl.BlockSpec((None, tb, width), lambda b, t: (b, t, 0))


def _mixer_call(kernel_fn, name, x, consts, out_width, scratch_shapes, tb):
    bsz, seq, d = x.shape
    return pl.pallas_call(
        kernel_fn,
        grid=(bsz, seq // tb),
        in_specs=[_tok_spec(tb, d)] + [_const_spec(c.shape) for c in consts],
        out_specs=_tok_spec(tb, out_width),
        out_shape=jax.ShapeDtypeStruct((bsz, seq, out_width), BF16),
        scratch_shapes=scratch_shapes,
        compiler_params=pltpu.CompilerParams(
            dimension_semantics=("parallel", "arbitrary"),
            vmem_limit_bytes=VMEM_LIMIT_BYTES),
        name=name,
    )(x, *consts)


def _row(v):
    return v.reshape(1, -1).astype(F32)


def _layer(x, lb, p, final_norm_w, final_norm):
    bsz, seq, d = x.shape
    tb = min(TOKEN_BLOCK, seq)
    assert seq % tb == 0 and tb % CHUNK == 0
    w_in = p['w_in']
    o = 0
    ssd_cols = SSD_CONV_DIM + SSD_HEADS + SSD_WIDTH
    w_ssd_raw = w_in[:, o:o + ssd_cols]
    o += ssd_cols
    w_ml = w_in[:, o:o + 3 * ML_WIDTH].astype(BF16)
    o += 3 * ML_WIDTH
    w_hg = w_in[:, o:o + 4 * HG_WIDTH].astype(BF16)
    o += 4 * HG_WIDTH
    w_gate = w_in[:, o:].astype(BF16)
    w_ssd = jnp.concatenate([
        w_ssd_raw[:, 0:SSD_CONV_DIM],
        jnp.pad(w_ssd_raw[:, SSD_CONV_DIM:SSD_CONV_DIM + SSD_HEADS], ((0, 0), (0, LANES - SSD_HEADS))),
        w_ssd_raw[:, SSD_CONV_DIM + SSD_HEADS:]], axis=1).astype(BF16)
    nw = _row(p['norm_w'])

    pad_h = (0, LANES - SSD_HEADS)
    ssd_consts = [
        nw, w_ssd, p['ssd_conv_w'].astype(F32), _row(p['ssd_conv_b']),
        _row(jnp.pad(p['ssd_dt_bias'], pad_h)), _row(jnp.pad(p['ssd_a_log'], pad_h)),
        _row(jnp.repeat(p['ssd_d'], SSD_HEAD_DIM)), _row(p['ssd_norm_w'])]
    y_ssd = _mixer_call(
        _ssd_kernel, 'ssd_mixer', x, ssd_consts, SSD_WIDTH,
        [pltpu.VMEM((tb + SUBLANES, SSD_CONV_DIM), F32), pltpu.VMEM((tb, SSD_CONV_DIM), F32),
         pltpu.VMEM((tb, LANES), F32), pltpu.VMEM((tb, SSD_WIDTH), F32),
         pltpu.VMEM((SSD_GROUPS, SSD_STATE, SSD_WIDTH // SSD_GROUPS), F32)], tb)

    wqk = jnp.concatenate([p['ml_wq'], p['ml_wk']], axis=-1).astype(BF16)
    wv = p['ml_wv'].astype(BF16)
    wif = jnp.pad(p['ml_w_if'].reshape(ML_HEADS, 3 * ML_HEAD_DIM, 2 * ML_HEADS),
                  ((0, 0), (0, 0), (0, LANES - 2 * ML_HEADS))).astype(BF16)
    bif = _row(jnp.pad(p['ml_b_if'], (0, LANES - 2 * ML_HEADS)))
    ml_consts = [nw, w_ml, p['ml_conv_w'].astype(F32), _row(p['ml_conv_b']), wqk, wv, wif, bif,
                 _row(p['ml_norm_w']), _row(p['ml_skip'])]
    y_ml = _mixer_call(
        _ml_kernel, 'mlstm_mixer', x, ml_consts, ML_WIDTH,
        [pltpu.VMEM((tb + SUBLANES, ML_WIDTH), F32), pltpu.VMEM((tb, ML_WIDTH), F32),
         pltpu.VMEM((tb, 2 * ML_WIDTH), F32),
         pltpu.VMEM((ML_HEADS, ML_HEAD_DIM, ML_HEAD_DIM), F32),
         pltpu.VMEM((SUBLANES, LANES), F32), pltpu.VMEM((SUBLANES, LANES), F32)], tb)

    hg_consts = [nw, w_hg, _row(lb), _row(p['hg_norm_w']), jnp.asarray(_hg_level_table())]
    y_hg = _mixer_call(
        _hg_kernel, 'hgrn2_mixer', x, hg_consts, HG_WIDTH,
        [pltpu.VMEM((tb, 4 * HG_WIDTH), F32), pltpu.VMEM((CHUNK, HG_WIDTH), F32),
         pltpu.VMEM((HG_HEADS, HG_HEAD_DIM, HG_HEAD_DIM), F32)], tb)

    merge_consts_a = [nw, w_gate]
    merge_consts_b = [p['w_branch_ssd'].astype(BF16), p['w_branch_ml'].astype(BF16),
                      p['w_branch_hg'].astype(BF16), p['w_out'].astype(BF16), _row(final_norm_w)]
    return pl.pallas_call(
        functools.partial(_merge_kernel, final_norm=final_norm),
        grid=(bsz, seq // tb),
        in_specs=([_tok_spec(tb, d)] + [_const_spec(c.shape) for c in merge_consts_a]
                  + [_tok_spec(tb, SSD_WIDTH), _tok_spec(tb, ML_WIDTH), _tok_spec(tb, HG_WIDTH)]
                  + [_const_spec(c.shape) for c in merge_consts_b]),
        out_specs=_tok_spec(tb, d),
        out_shape=jax.ShapeDtypeStruct((bsz, seq, d), F32),
        compiler_params=pltpu.CompilerParams(
            dimension_semantics=("parallel", "parallel"),
            vmem_limit_bytes=VMEM_LIMIT_BYTES),
        name='merge_out',
    )(x, *merge_consts_a, y_ssd, y_ml, y_hg, *merge_consts_b)


_LAYER_PARAMS = ('norm_w', 'w_in', 'ssd_conv_w', 'ssd_conv_b', 'ssd_dt_bias', 'ssd_a_log', 'ssd_d',
                 'ssd_norm_w', 'ml_conv_w', 'ml_conv_b', 'ml_wq', 'ml_wk', 'ml_wv', 'ml_w_if',
                 'ml_b_if', 'ml_norm_w', 'ml_skip', 'hg_norm_w', 'w_branch_ssd', 'w_branch_ml',
                 'w_branch_hg', 'w_out')


def kernel(x, norm_w, w_in, ssd_conv_w, ssd_conv_b, ssd_dt_bias, ssd_a_log, ssd_d, ssd_norm_w, ml_conv_w, ml_conv_b, ml_wq, ml_wk, ml_wv, ml_w_if, ml_b_if, ml_norm_w, ml_skip, hg_lower_bounds, hg_norm_w, w_branch_ssd, w_branch_ml, w_branch_hg, w_out, final_norm_w):
    stacked = dict(norm_w=norm_w, w_in=w_in, ssd_conv_w=ssd_conv_w, ssd_conv_b=ssd_conv_b,
                   ssd_dt_bias=ssd_dt_bias, ssd_a_log=ssd_a_log, ssd_d=ssd_d, ssd_norm_w=ssd_norm_w,
                   ml_conv_w=ml_conv_w, ml_conv_b=ml_conv_b, ml_wq=ml_wq, ml_wk=ml_wk, ml_wv=ml_wv,
                   ml_w_if=ml_w_if, ml_b_if=ml_b_if, ml_norm_w=ml_norm_w, ml_skip=ml_skip,
                   hg_norm_w=hg_norm_w, w_branch_ssd=w_branch_ssd, w_branch_ml=w_branch_ml,
                   w_branch_hg=w_branch_hg, w_out=w_out)
    depth = norm_w.shape[0]
    lbs = jnp.cumsum(jax.nn.softmax(hg_lower_bounds.astype(F32), axis=0), axis=0)
    lbs = lbs - lbs[0]
    for l in range(depth):
        p = {k: stacked[k][l] for k in _LAYER_PARAMS}
        x = _layer(x, lbs[l], p, final_norm_w, final_norm=(l == depth - 1))
    return x
```

```python
import functools

import numpy as np
import jax
import jax.numpy as jnp
from jax import lax
from jax.experimental import pallas as pl
from jax.experimental.pallas import tpu as pltpu

F32 = jnp.float32
BF16 = jnp.bfloat16

EPS = 1e-6
CONV_K = 4
LANES = 128
SUBLANES = 8
CHUNK = 128
TOKEN_BLOCK = 512
VMEM_LIMIT_BYTES = 56 * 1024 * 1024

SSD_HEAD_DIM = 64
SSD_HEADS = 16
SSD_GROUPS = 2
SSD_STATE = 128
SSD_WIDTH = SSD_HEADS * SSD_HEAD_DIM
SSD_CONV_DIM = SSD_WIDTH + 2 * SSD_GROUPS * SSD_STATE
ML_HEADS = 4
ML_HEAD_DIM = 128
ML_WIDTH = ML_HEADS * ML_HEAD_DIM
HG_HEADS = 4
HG_HEAD_DIM = 128
HG_WIDTH = HG_HEADS * HG_HEAD_DIM
HG_LEVELS = 7


def _dot(a, b):
    return jnp.dot(a, b, preferred_element_type=F32)


def _dot_nt(a, b):
    return lax.dot_general(a, b, (((1,), (1,)), ((), ())), preferred_element_type=F32)


def _rmsnorm(x, w):
    return x * lax.rsqrt(jnp.mean(x * x, axis=-1, keepdims=True) + EPS) * w


def _softplus(x):
    return jnp.maximum(x, 0.0) + jnp.log1p(jnp.exp(-jnp.abs(x)))


def _log_sigmoid(x):
    return jnp.minimum(x, 0.0) - jnp.log1p(jnp.exp(-jnp.abs(x)))


def _silu(x):
    return x * jax.nn.sigmoid(x)


def _tril_ones_bf16(n):
    r = lax.broadcasted_iota(jnp.int32, (n, n), 0)
    c = lax.broadcasted_iota(jnp.int32, (n, n), 1)
    return jnp.where(c <= r, 1.0, 0.0).astype(BF16)


def _cumsum_time(x, tri):
    hi = x.astype(BF16)
    r1 = x - hi.astype(F32)
    mid = r1.astype(BF16)
    lo = (r1 - mid.astype(F32)).astype(BF16)
    return _dot(tri, hi) + _dot(tri, mid) + _dot(tri, lo)


def _colb(x, j, n=LANES):
    return jnp.broadcast_to(x[:, j:j + 1], (x.shape[0], n))


def _rowb(x, j, m):
    return jnp.broadcast_to(x[j:j + 1, :], (m, x.shape[1]))


def _expand_heads(v):
    rows = v.shape[0]
    lane = lax.broadcasted_iota(jnp.int32, (rows, LANES), 1)
    parts = []
    for j in range(SSD_HEADS // 2):
        a = _colb(v, 2 * j)
        b = _colb(v, 2 * j + 1)
        parts.append(jnp.where(lane < SSD_HEAD_DIM, a, b))
    return jnp.concatenate(parts, axis=1)


def _chunk_start(c):
    return c * CHUNK if isinstance(c, int) else pl.multiple_of(c * CHUNK, CHUNK)


def _for_chunks(n, body):
    for c in range(n):
        body(c, 0)


def _store_col_blocks(dst_ref, first_block, row0, val):
    for i in range(val.shape[1] // LANES):
        dst_ref[first_block + i, pl.ds(row0, val.shape[0]), :] = val[:, i * LANES:(i + 1) * LANES]


def _causal_conv_silu(pre_ref, cw_ref, cb_ref, out_ref, tb):
    for blk in range(pre_ref.shape[0]):
        cols = slice(blk * LANES, (blk + 1) * LANES)
        acc = cb_ref[:, cols] + cw_ref[0:1, cols] * pre_ref[blk, pl.ds(SUBLANES - 3, tb), :]
        for j in range(1, CONV_K):
            acc = acc + cw_ref[j:j + 1, cols] * pre_ref[blk, pl.ds(SUBLANES - 3 + j, tb), :]
        out_ref[:, cols] = _silu(acc)
        pre_ref[blk, 0:SUBLANES, :] = pre_ref[blk, pl.ds(tb, SUBLANES), :]


def _ssd_kernel(x_ref, nw_ref, w_ref, cw_ref, cb_ref, dtb_ref, alog_ref, dskip_ref, gnw_ref,
                y_ref, pre_ref, xc_ref, dt_ref, z_ref, st_ref):
    tb = x_ref.shape[0]
    L = CHUNK

    @pl.when(pl.program_id(1) == 0)
    def _():
        pre_ref[:, 0:SUBLANES, :] = jnp.zeros((pre_ref.shape[0], SUBLANES, LANES), F32)
        st_ref[...] = jnp.zeros(st_ref.shape, F32)

    hn = _rmsnorm(x_ref[...], nw_ref[...]).astype(BF16)
    group = 4 * LANES
    for c0 in range(0, SSD_CONV_DIM, group):
        _store_col_blocks(pre_ref, c0 // LANES, SUBLANES, _dot(hn, w_ref[:, c0:c0 + group]))
    dt_ref[...] = _dot(hn, w_ref[:, SSD_CONV_DIM:SSD_CONV_DIM + LANES])
    z_ref[...] = _dot(hn, w_ref[:, SSD_CONV_DIM + LANES:])
    _causal_conv_silu(pre_ref, cw_ref, cb_ref, xc_ref, tb)

    lane_row = lax.broadcasted_iota(jnp.int32, (1, LANES), 1)
    a_row = jnp.where(lane_row < SSD_HEADS, -jnp.exp(alog_ref[...]), 0.0)
    r_i = lax.broadcasted_iota(jnp.int32, (L, L), 0)
    c_i = lax.broadcasted_iota(jnp.int32, (L, L), 1)
    causal = c_i <= r_i
    lane = lax.broadcasted_iota(jnp.int32, (L, LANES), 1)
    lo_half = lane < SSD_HEAD_DIM
    tri = _tril_ones_bf16(L)
    gs = SSD_WIDTH // SSD_GROUPS
    pairs_per_group = SSD_HEADS // SSD_GROUPS // 2

    def chunk(c, carry):
        r0 = _chunk_start(c)
        rows = pl.ds(r0, L)
        xs = xc_ref[rows, 0:SSD_WIDTH]
        dt = _softplus(dt_ref[rows, :] + dtb_ref[...])
        cum = _cumsum_time(dt * a_row, tri)
        cum_last = cum[L - 1:L, :]
        cum_t = cum.T
        dt_t = dt.T
        dtw_t = (dt * jnp.exp(cum_last - cum)).T
        dec_last_x = _expand_heads(jnp.exp(cum_last))
        xs_b = xs.astype(BF16)

        y_parts = []
        for g in range(SSD_GROUPS):
            bm = xc_ref[rows, SSD_WIDTH + g * SSD_STATE:SSD_WIDTH + (g + 1) * SSD_STATE]
            cm = xc_ref[rows, SSD_WIDTH + (SSD_GROUPS + g) * SSD_STATE:
                        SSD_WIDTH + (SSD_GROUPS + g + 1) * SSD_STATE]
            cm_b = cm.astype(BF16)
            cb = _dot_nt(cm_b, bm.astype(BF16))
            bm_t = bm.T
            h_t = st_ref[g]
            y_off = _dot(cm_b, h_t.astype(BF16))
            new_state = []
            for jp in range(pairs_per_group):
                pair = g * pairs_per_group + jp
                lanes = slice(pair * LANES, (pair + 1) * LANES)
                sc, bsc, dec = [], [], []
                for h in (2 * pair, 2 * pair + 1):
                    c_col = _colb(cum, h)
                    seg = c_col - _rowb(cum_t, h, L)
                    lm = jnp.exp(jnp.where(causal, seg, -jnp.inf))
                    sc.append((cb * lm * _rowb(dt_t, h, L)).astype(BF16))
                    bsc.append((bm_t * _rowb(dtw_t, h, L)).astype(BF16))
                    dec.append(jnp.exp(c_col))
                xp = xs_b[:, lanes]
                zero = jnp.zeros_like(xp)
                rhs = jnp.concatenate([jnp.where(lo_half, xp, zero), jnp.where(lo_half, zero, xp)], axis=0)
                y_diag = _dot(jnp.concatenate(sc, axis=1), rhs)
                local = _dot(jnp.concatenate(bsc, axis=1), rhs)
                y_parts.append(y_diag + y_off[:, jp * LANES:(jp + 1) * LANES]
                               * jnp.where(lo_half, dec[0], dec[1]))
                new_state.append(h_t[:, jp * LANES:(jp + 1) * LANES] * dec_last_x[:, lanes] + local)
            st_ref[g] = jnp.concatenate(new_state, axis=1)
        y = jnp.concatenate(y_parts, axis=1) + xs * dskip_ref[...]
        y = y * _silu(z_ref[rows, :])
        outs = []
        for g in range(SSD_GROUPS):
            yg = y[:, g * gs:(g + 1) * gs]
            outs.append(yg * lax.rsqrt(jnp.mean(yg * yg, axis=-1, keepdims=True) + EPS))
        y_ref[rows, :] = (jnp.concatenate(outs, axis=1) * gnw_ref[...]).astype(BF16)
        return carry

    _for_chunks(tb // L, chunk)


def _ml_kernel(x_ref, nw_ref, w_ref, cw_ref, cb_ref, wqk_ref, wv_ref, wif_ref, bif_ref, lnw_ref,
               skip_ref, y_ref, pre_ref, mc_ref, oz_ref, qkv_ref, if_ref, ct_ref, n_ref, m_ref):
    tb = x_ref.shape[0]
    L = CHUNK
    D = ML_HEAD_DIM

    @pl.when(pl.program_id(1) == 0)
    def _():
        pre_ref[:, 0:SUBLANES, :] = jnp.zeros((pre_ref.shape[0], SUBLANES, LANES), F32)
        ct_ref[...] = jnp.zeros(ct_ref.shape, F32)
        n_ref[...] = jnp.zeros(n_ref.shape, F32)
        m_ref[...] = jnp.zeros(m_ref.shape, F32)

    hn = _rmsnorm(x_ref[...], nw_ref[...]).astype(BF16)
    _store_col_blocks(pre_ref, 0, SUBLANES, _dot(hn, w_ref[:, 0:ML_WIDTH]))
    oz_ref[...] = _dot(hn, w_ref[:, ML_WIDTH:])
    for h in range(ML_HEADS):
        v = _dot(pre_ref[h, pl.ds(SUBLANES, tb), :].astype(BF16), wv_ref[h])
        qkv_ref[:, h * 3 * D + 2 * D:(h + 1) * 3 * D] = v
    _causal_conv_silu(pre_ref, cw_ref, cb_ref, mc_ref, tb)
    for h in range(ML_HEADS):
        qkv_ref[:, h * 3 * D:h * 3 * D + 2 * D] = _dot(mc_ref[:, h * D:(h + 1) * D].astype(BF16), wqk_ref[h])
    if_ref[...] = _dot(qkv_ref[...].astype(BF16), wif_ref[...]) + bif_ref[...]

    r_i = lax.broadcasted_iota(jnp.int32, (L, L), 0)
    c_i = lax.broadcasted_iota(jnp.int32, (L, L), 1)
    causal = c_i <= r_i
    tri = _tril_ones_bf16(L)
    scale = D ** -0.5

    def chunk(c, carry):
        r0 = _chunk_start(c)
        rows = pl.ds(r0, L)
        if_pre = if_ref[rows, :]
        cum = _cumsum_time(_log_sigmoid(if_pre), tri)
        cum_t = cum.T
        if_t = if_pre.T
        for h in range(ML_HEADS):
            cols = slice(h * D, (h + 1) * D)
            q = qkv_ref[rows, h * 3 * D:h * 3 * D + D] * scale
            k = qkv_ref[rows, h * 3 * D + D:h * 3 * D + 2 * D]
            v = qkv_ref[rows, h * 3 * D + 2 * D:(h + 1) * 3 * D]
            q_b = q.astype(BF16)
            k_b = k.astype(BF16)
            c_col = _colb(cum, ML_HEADS + h)
            c_row = _rowb(cum_t, ML_HEADS + h, L)
            i_row = _rowb(if_t, h, L)
            i_col = _colb(if_pre, h)
            m_in = m_ref[h:h + 1, :]
            n_in = n_ref[h:h + 1, :]
            ct_in = ct_ref[h]
            log_d = jnp.where(causal, c_col - c_row + i_row, -jnp.inf)
            log_inter = c_col + m_in
            m_t = jnp.maximum(log_inter, jnp.max(log_d, axis=1, keepdims=True))
            w_intra = jnp.exp(log_d - m_t)
            w_inter = jnp.exp(log_inter - m_t)
            s = _dot_nt(q_b, k_b) * w_intra
            num = _dot(s.astype(BF16), v.astype(BF16)) + w_inter * _dot(q_b, ct_in.astype(BF16))
            den = (jnp.sum(s, axis=1, keepdims=True)
                   + w_inter * jnp.sum(q * n_in, axis=1, keepdims=True))
            den = jnp.maximum(jnp.abs(den), jnp.exp(-m_t))
            hh = num / den
            g_row = c_col[L - 1:L, :]
            log_end = g_row - c_col + i_col
            m_loc = jnp.max(log_end, axis=0, keepdims=True)
            w_end = jnp.exp(log_end - m_loc)
            ct_loc = _dot(k.T.astype(BF16), (v * w_end).astype(BF16))
            n_loc = jnp.sum(k * w_end, axis=0, keepdims=True)
            m_new = jnp.maximum(g_row + m_in, m_loc)
            a_old = jnp.exp(g_row + m_in - m_new)
            a_loc = jnp.exp(m_loc - m_new)
            ct_ref[h] = a_old * ct_in + a_loc * ct_loc
            n_ref[h:h + 1, :] = a_old * n_in + a_loc * n_loc
            m_ref[h:h + 1, :] = m_new
            mu = jnp.mean(hh, axis=-1, keepdims=True)
            xc = hh - mu
            ln = xc * lax.rsqrt(jnp.mean(xc * xc, axis=-1, keepdims=True) + EPS) * lnw_ref[:, cols]
            o_gate = jax.nn.sigmoid(oz_ref[rows, h * D:(h + 1) * D])
            z = oz_ref[rows, ML_WIDTH + h * D:ML_WIDTH + (h + 1) * D]
            out = (ln * o_gate + skip_ref[:, cols] * mc_ref[rows, cols]) * _silu(z)
            y_ref[rows, cols] = out.astype(BF16)
        return carry

    _for_chunks(tb // L, chunk)


def _hg_level_table():
    l = np.arange(CHUNK)[:, None]
    s = np.arange(CHUNK)[None, :]
    x = l ^ s
    msb = np.floor(np.log2(np.maximum(x, 1))).astype(np.int32)
    return np.where(s < l, msb, np.where(s == l, HG_LEVELS, HG_LEVELS + 1)).astype(np.int32)


def _replicated_row(ref, blk, r):
    return ref[blk, pl.ds(r, SUBLANES, stride=0), :]


def _hg_reference_rows(cum_ref, level):
    b = 1 << level
    sub = lax.broadcasted_iota(jnp.int32, (SUBLANES, LANES), 0)
    col_blocks = []
    for blk in range(cum_ref.shape[0]):
        def row8(r, blk=blk):
            return _replicated_row(cum_ref, blk, r)
        pieces = []
        if 2 * b >= SUBLANES:
            for i in range(CHUNK // (2 * b)):
                pieces.extend([row8(i * 2 * b + b - 1)] * (2 * b // SUBLANES))
        else:
            for grp in range(CHUNK // SUBLANES):
                base = grp * SUBLANES
                acc = row8(base + b - 1)
                for i in range(1, SUBLANES // (2 * b)):
                    acc = jnp.where(sub >= i * 2 * b, row8(base + i * 2 * b + b - 1), acc)
                pieces.append(acc)
        col_blocks.append(jnp.concatenate(pieces, axis=0))
    return jnp.concatenate(col_blocks, axis=1)


def _hg_kernel(x_ref, nw_ref, w_ref, lb_ref, gnw_ref, lv_ref, y_ref, pj_ref, cum_ref, st_ref):
    tb = x_ref.shape[0]
    L = CHUNK
    D = HG_HEAD_DIM
    W = HG_WIDTH

    @pl.when(pl.program_id(1) == 0)
    def _():
        st_ref[...] = jnp.zeros(st_ref.shape, F32)

    hn = _rmsnorm(x_ref[...], nw_ref[...]).astype(BF16)
    pj_ref[...] = _dot(hn, w_ref[...])
    tri = _tril_ones_bf16(L)
    lb = lb_ref[...]
    log_lb = jnp.log(lb)

    def chunk(c, carry):
        r0 = _chunk_start(c)
        rows = pl.ds(r0, L)
        fx = pj_ref[rows, W:2 * W]
        ls_pos = _log_sigmoid(fx)
        b_term = log_lb + (ls_pos - fx)
        log_f = jnp.maximum(ls_pos, b_term) + jnp.log1p(jnp.exp(-jnp.abs(ls_pos - b_term)))
        k = (1.0 - lb) * jax.nn.sigmoid(-fx)
        q = _silu(pj_ref[rows, 0:W])
        v_b = pj_ref[rows, 2 * W:3 * W].astype(BF16)
        cum = _cumsum_time(log_f, tri)
        _store_col_blocks(cum_ref, 0, 0, cum)
        lv = lv_ref[...]
        q_b = q.astype(BF16)
        k_b = k.astype(BF16)
        attn = []
        for h in range(HG_HEADS):
            cols = slice(h * D, (h + 1) * D)
            attn.append(jnp.where(lv == HG_LEVELS, _dot_nt(q_b[:, cols], k_b[:, cols]), 0.0))
        for level in range(HG_LEVELS):
            e = jnp.exp(-jnp.abs(cum - _hg_reference_rows(cum_ref, level)))
            qe = (q * e).astype(BF16)
            ke = (k * e).astype(BF16)
            for h in range(HG_HEADS):
                cols = slice(h * D, (h + 1) * D)
                attn[h] = jnp.where(lv == level, _dot_nt(qe[:, cols], ke[:, cols]), attn[h])
        cum_last = cum[L - 1:L, :]
        q_dec = (q * jnp.exp(cum)).astype(BF16)
        k_end = k * jnp.exp(cum_last - cum)
        dec_last = jnp.exp(cum_last)
        gz = _silu(pj_ref[rows, 3 * W:4 * W])
        for h in range(HG_HEADS):
            cols = slice(h * D, (h + 1) * D)
            st = st_ref[h]
            o = _dot(attn[h].astype(BF16), v_b[:, cols]) + _dot_nt(q_dec[:, cols], st.astype(BF16))
            v_t = pj_ref[rows, 2 * W + h * D:2 * W + (h + 1) * D].T.astype(BF16)
            st_ref[h] = st * dec_last[:, cols] + _dot(v_t, k_end[:, cols].astype(BF16))
            on = o * lax.rsqrt(jnp.mean(o * o, axis=-1, keepdims=True) + EPS)
            y_ref[rows, cols] = (on * gnw_ref[:, cols] * gz[:, cols]).astype(BF16)
        return carry

    _for_chunks(tb // L, chunk)


def _merge_kernel(x_ref, nw_ref, wg_ref, ys_ref, ym_ref, yh_ref, wbs_ref, wbm_ref, wbh_ref, wo_ref,
                  fnw_ref, o_ref, *, final_norm):
    x = x_ref[...]
    hn = _rmsnorm(x, nw_ref[...]).astype(BF16)
    d = x.shape[-1]
    merged = jax.nn.sigmoid(_dot(hn, wg_ref[:, 0:d])) * _dot(ys_ref[...], wbs_ref[...])
    merged = merged + jax.nn.sigmoid(_dot(hn, wg_ref[:, d:2 * d])) * _dot(ym_ref[...], wbm_ref[...])
    merged = merged + jax.nn.sigmoid(_dot(hn, wg_ref[:, 2 * d:3 * d])) * _dot(yh_ref[...], wbh_ref[...])
    out = x + _dot(merged.astype(BF16), wo_ref[...])
    if final_norm:
        out = _rmsnorm(out, fnw_ref[...])
    o_ref[...] = out


def _const_spec(shape):
    nd = len(shape)
    return pl.BlockSpec(shape, lambda b, t: (0,) * nd)


def _tok_spec(tb, width):
    return pl.BlockSpec((None, tb, width), lambda b, t: (b, t, 0))


def _mixer_call(kernel_fn, name, x, consts, out_width, scratch_shapes, tb):
    bsz, seq, d = x.shape
    return pl.pallas_call(
        kernel_fn,
        grid=(bsz, seq // tb),
        in_specs=[_tok_spec(tb, d)] + [_const_spec(c.shape) for c in consts],
        out_specs=_tok_spec(tb, out_width),
        out_shape=jax.ShapeDtypeStruct((bsz, seq, out_width), BF16),
        scratch_shapes=scratch_shapes,
        compiler_params=pltpu.CompilerParams(
            dimension_semantics=("parallel", "arbitrary"),
            vmem_limit_bytes=VMEM_LIMIT_BYTES),
        name=name,
    )(x, *consts)


def _row(v):
    return v.reshape(1, -1).astype(F32)


def _layer(x, lb, p, final_norm_w, final_norm):
    bsz, seq, d = x.shape
    tb = min(TOKEN_BLOCK, seq)
    assert seq % tb == 0 and tb % CHUNK == 0
    w_in = p['w_in']
    o = 0
    ssd_cols = SSD_CONV_DIM + SSD_HEADS + SSD_WIDTH
    w_ssd_raw = w_in[:, o:o + ssd_cols]
    o += ssd_cols
    w_ml = w_in[:, o:o + 3 * ML_WIDTH].astype(BF16)
    o += 3 * ML_WIDTH
    w_hg = w_in[:, o:o + 4 * HG_WIDTH].astype(BF16)
    o += 4 * HG_WIDTH
    w_gate = w_in[:, o:].astype(BF16)
    w_ssd = jnp.concatenate([
        w_ssd_raw[:, 0:SSD_CONV_DIM],
        jnp.pad(w_ssd_raw[:, SSD_CONV_DIM:SSD_CONV_DIM + SSD_HEADS], ((0, 0), (0, LANES - SSD_HEADS))),
        w_ssd_raw[:, SSD_CONV_DIM + SSD_HEADS:]], axis=1).astype(BF16)
    nw = _row(p['norm_w'])

    pad_h = (0, LANES - SSD_HEADS)
    ssd_consts = [
        nw, w_ssd, p['ssd_conv_w'].astype(F32), _row(p['ssd_conv_b']),
        _row(jnp.pad(p['ssd_dt_bias'], pad_h)), _row(jnp.pad(p['ssd_a_log'], pad_h)),
        _row(jnp.repeat(p['ssd_d'], SSD_HEAD_DIM)), _row(p['ssd_norm_w'])]
    y_ssd = _mixer_call(
        _ssd_kernel, 'ssd_mixer', x, ssd_consts, SSD_WIDTH,
        [pltpu.VMEM((SSD_CONV_DIM // LANES, tb + SUBLANES, LANES), F32),
         pltpu.VMEM((tb, SSD_CONV_DIM), F32),
         pltpu.VMEM((tb, LANES), F32), pltpu.VMEM((tb, SSD_WIDTH), F32),
         pltpu.VMEM((SSD_GROUPS, SSD_STATE, SSD_WIDTH // SSD_GROUPS), F32)], tb)

    wqk = jnp.concatenate([p['ml_wq'], p['ml_wk']], axis=-1).astype(BF16)
    wv = p['ml_wv'].astype(BF16)
    wif = jnp.pad(p['ml_w_if'], ((0, 0), (0, LANES - 2 * ML_HEADS))).astype(BF16)
    bif = _row(jnp.pad(p['ml_b_if'], (0, LANES - 2 * ML_HEADS)))
    ml_consts = [nw, w_ml, p['ml_conv_w'].astype(F32), _row(p['ml_conv_b']), wqk, wv, wif, bif,
                 _row(p['ml_norm_w']), _row(p['ml_skip'])]
    y_ml = _mixer_call(
        _ml_kernel, 'mlstm_mixer', x, ml_consts, ML_WIDTH,
        [pltpu.VMEM((ML_WIDTH // LANES, tb + SUBLANES, LANES), F32), pltpu.VMEM((tb, ML_WIDTH), F32),
         pltpu.VMEM((tb, 2 * ML_WIDTH), F32), pltpu.VMEM((tb, 3 * ML_WIDTH), F32),
         pltpu.VMEM((tb, LANES), F32),
         pltpu.VMEM((ML_HEADS, ML_HEAD_DIM, ML_HEAD_DIM), F32),
         pltpu.VMEM((SUBLANES, LANES), F32), pltpu.VMEM((SUBLANES, LANES), F32)], tb)

    hg_consts = [nw, w_hg, _row(lb), _row(p['hg_norm_w']), jnp.asarray(_hg_level_table())]
    y_hg = _mixer_call(
        _hg_kernel, 'hgrn2_mixer', x, hg_consts, HG_WIDTH,
        [pltpu.VMEM((tb, 4 * HG_WIDTH), F32), pltpu.VMEM((HG_WIDTH // LANES, CHUNK, LANES), F32),
         pltpu.VMEM((HG_HEADS, HG_HEAD_DIM, HG_HEAD_DIM), F32)], tb)

    merge_consts_a = [nw, w_gate]
    merge_consts_b = [p['w_branch_ssd'].astype(BF16), p['w_branch_ml'].astype(BF16),
                      p['w_branch_hg'].astype(BF16), p['w_out'].astype(BF16), _row(final_norm_w)]
    return pl.pallas_call(
        functools.partial(_merge_kernel, final_norm=final_norm),
        grid=(bsz, seq // tb),
        in_specs=([_tok_spec(tb, d)] + [_const_spec(c.shape) for c in merge_consts_a]
                  + [_tok_spec(tb, SSD_WIDTH), _tok_spec(tb, ML_WIDTH), _tok_spec(tb, HG_WIDTH)]
                  + [_const_spec(c.shape) for c in merge_consts_b]),
        out_specs=_tok_spec(tb, d),
        out_shape=jax.ShapeDtypeStruct((bsz, seq, d), F32),
        compiler_params=pltpu.CompilerParams(
            dimension_semantics=("parallel", "parallel"),
            vmem_limit_bytes=VMEM_LIMIT_BYTES),
        name='merge_out',
    )(x, *merge_consts_a, y_ssd, y_ml, y_hg, *merge_consts_b)


_LAYER_PARAMS = ('norm_w', 'w_in', 'ssd_conv_w', 'ssd_conv_b', 'ssd_dt_bias', 'ssd_a_log', 'ssd_d',
                 'ssd_norm_w', 'ml_conv_w', 'ml_conv_b', 'ml_wq', 'ml_wk', 'ml_wv', 'ml_w_if',
                 'ml_b_if', 'ml_norm_w', 'ml_skip', 'hg_norm_w', 'w_branch_ssd', 'w_branch_ml',
                 'w_branch_hg', 'w_out')


def kernel(x, norm_w, w_in, ssd_conv_w, ssd_conv_b, ssd_dt_bias, ssd_a_log, ssd_d, ssd_norm_w, ml_conv_w, ml_conv_b, ml_wq, ml_wk, ml_wv, ml_w_if, ml_b_if, ml_norm_w, ml_skip, hg_lower_bounds, hg_norm_w, w_branch_ssd, w_branch_ml, w_branch_hg, w_out, final_norm_w):
    stacked = dict(norm_w=norm_w, w_in=w_in, ssd_conv_w=ssd_conv_w, ssd_conv_b=ssd_conv_b,
                   ssd_dt_bias=ssd_dt_bias, ssd_a_log=ssd_a_log, ssd_d=ssd_d, ssd_norm_w=ssd_norm_w,
                   ml_conv_w=ml_conv_w, ml_conv_b=ml_conv_b, ml_wq=ml_wq, ml_wk=ml_wk, ml_wv=ml_wv,
                   ml_w_if=ml_w_if, ml_b_if=ml_b_if, ml_norm_w=ml_norm_w, ml_skip=ml_skip,
                   hg_norm_w=hg_norm_w, w_branch_ssd=w_branch_ssd, w_branch_ml=w_branch_ml,
                   w_branch_hg=w_branch_hg, w_out=w_out)
    depth = norm_w.shape[0]
    lbs = jnp.cumsum(jax.nn.softmax(hg_lower_bounds.astype(F32), axis=0), axis=0)
    lbs = lbs - lbs[0]
    for l in range(depth):
        p = {k: stacked[k][l] for k in _LAYER_PARAMS}
        x = _layer(x, lbs[l], p, final_norm_w, final_norm=(l == depth - 1))
    return x
```

```python
import functools

import numpy as np
import jax
import jax.numpy as jnp
from jax import lax
from jax.experimental import pallas as pl
from jax.experimental.pallas import tpu as pltpu

F32 = jnp.float32
BF16 = jnp.bfloat16

EPS = 1e-6
CONV_K = 4
LANES = 128
SUBLANES = 8
CHUNK = 128
TOKEN_BLOCK = 512
PROJ_ROWS = 256
VMEM_LIMIT_BYTES = 56 * 1024 * 1024

SSD_HEAD_DIM = 64
SSD_HEADS = 16
SSD_GROUPS = 2
SSD_STATE = 128
SSD_WIDTH = SSD_HEADS * SSD_HEAD_DIM
SSD_CONV_DIM = SSD_WIDTH + 2 * SSD_GROUPS * SSD_STATE
ML_HEADS = 4
ML_HEAD_DIM = 128
ML_WIDTH = ML_HEADS * ML_HEAD_DIM
HG_HEADS = 4
HG_HEAD_DIM = 128
HG_WIDTH = HG_HEADS * HG_HEAD_DIM
HG_LEVELS = 7


def _dot(a, b):
    return jnp.dot(a, b, preferred_element_type=F32)


def _dot_nt(a, b):
    return lax.dot_general(a, b, (((1,), (1,)), ((), ())), preferred_element_type=F32)


def _rmsnorm(x, w):
    return x * lax.rsqrt(jnp.mean(x * x, axis=-1, keepdims=True) + EPS) * w


def _softplus(x):
    return jnp.maximum(x, 0.0) + jnp.log1p(jnp.exp(-jnp.abs(x)))


def _log_sigmoid(x):
    return jnp.minimum(x, 0.0) - jnp.log1p(jnp.exp(-jnp.abs(x)))


def _silu(x):
    return x * jax.nn.sigmoid(x)


def _tril_ones_bf16(n):
    r = lax.broadcasted_iota(jnp.int32, (n, n), 0)
    c = lax.broadcasted_iota(jnp.int32, (n, n), 1)
    return jnp.where(c <= r, 1.0, 0.0).astype(BF16)


def _cumsum_time(x, tri):
    hi = x.astype(BF16)
    r1 = x - hi.astype(F32)
    mid = r1.astype(BF16)
    lo = (r1 - mid.astype(F32)).astype(BF16)
    return _dot(tri, hi) + _dot(tri, mid) + _dot(tri, lo)


def _colb(x, j, n=LANES):
    return jnp.broadcast_to(x[:, j:j + 1], (x.shape[0], n))


def _rowb(x, j, m):
    return jnp.broadcast_to(x[j:j + 1, :], (m, x.shape[1]))


def _expand_heads(v):
    rows = v.shape[0]
    lane = lax.broadcasted_iota(jnp.int32, (rows, LANES), 1)
    parts = []
    for j in range(SSD_HEADS // 2):
        a = _colb(v, 2 * j)
        b = _colb(v, 2 * j + 1)
        parts.append(jnp.where(lane < SSD_HEAD_DIM, a, b))
    return jnp.concatenate(parts, axis=1)


def _chunk_start(c):
    return c * CHUNK if isinstance(c, int) else pl.multiple_of(c * CHUNK, CHUNK)


def _for_chunks(n, body):
    for c in range(n):
        body(c, 0)


def _store_col_blocks(dst_ref, first_block, row0, val):
    for i in range(val.shape[1] // LANES):
        dst_ref[first_block + i, pl.ds(row0, val.shape[0]), :] = val[:, i * LANES:(i + 1) * LANES]


def _causal_conv_silu(pre_ref, cw_ref, cb_ref, out_ref, r0, n):
    for blk in range(pre_ref.shape[0]):
        cols = slice(blk * LANES, (blk + 1) * LANES)
        acc = cb_ref[:, cols] + cw_ref[0:1, cols] * pre_ref[blk, pl.ds(r0 + SUBLANES - 3, n), :]
        for j in range(1, CONV_K):
            acc = acc + cw_ref[j:j + 1, cols] * pre_ref[blk, pl.ds(r0 + SUBLANES - 3 + j, n), :]
        out_ref[pl.ds(r0, n), cols] = _silu(acc)


def _conv_carry(pre_ref, tb):
    for blk in range(pre_ref.shape[0]):
        pre_ref[blk, 0:SUBLANES, :] = pre_ref[blk, pl.ds(tb, SUBLANES), :]


def _ssd_kernel(x_ref, nw_ref, w_ref, cw_ref, cb_ref, dtb_ref, alog_ref, dskip_ref, gnw_ref,
                y_ref, pre_ref, xc_ref, dt_ref, z_ref, st_ref):
    tb = x_ref.shape[0]
    L = CHUNK

    @pl.when(pl.program_id(1) == 0)
    def _():
        pre_ref[:, 0:SUBLANES, :] = jnp.zeros((pre_ref.shape[0], SUBLANES, LANES), F32)
        st_ref[...] = jnp.zeros(st_ref.shape, F32)

    group = 4 * LANES
    for r0 in range(0, tb, PROJ_ROWS):
        prow = pl.ds(r0, PROJ_ROWS)
        hn = _rmsnorm(x_ref[prow, :], nw_ref[...]).astype(BF16)
        for c0 in range(0, SSD_CONV_DIM, group):
            _store_col_blocks(pre_ref, c0 // LANES, SUBLANES + r0, _dot(hn, w_ref[:, c0:c0 + group]))
        dt_ref[prow, :] = _dot(hn, w_ref[:, SSD_CONV_DIM:SSD_CONV_DIM + LANES])
        z_ref[prow, :] = _dot(hn, w_ref[:, SSD_CONV_DIM + LANES:])
        _causal_conv_silu(pre_ref, cw_ref, cb_ref, xc_ref, r0, PROJ_ROWS)
    _conv_carry(pre_ref, tb)

    lane_row = lax.broadcasted_iota(jnp.int32, (1, LANES), 1)
    a_row = jnp.where(lane_row < SSD_HEADS, -jnp.exp(alog_ref[...]), 0.0)
    r_i = lax.broadcasted_iota(jnp.int32, (L, L), 0)
    c_i = lax.broadcasted_iota(jnp.int32, (L, L), 1)
    causal = c_i <= r_i
    lane = lax.broadcasted_iota(jnp.int32, (L, LANES), 1)
    lo_half = lane < SSD_HEAD_DIM
    tri = _tril_ones_bf16(L)
    gs = SSD_WIDTH // SSD_GROUPS
    pairs_per_group = SSD_HEADS // SSD_GROUPS // 2

    def chunk(c, carry):
        r0 = _chunk_start(c)
        rows = pl.ds(r0, L)
        xs = xc_ref[rows, 0:SSD_WIDTH]
        dt = _softplus(dt_ref[rows, :] + dtb_ref[...])
        cum = _cumsum_time(dt * a_row, tri)
        cum_last = cum[L - 1:L, :]
        cum_t = cum.T
        dt_t = dt.T
        dtw_t = (dt * jnp.exp(cum_last - cum)).T
        dec_last_x = _expand_heads(jnp.exp(cum_last))
        xs_b = xs.astype(BF16)

        y_parts = []
        for g in range(SSD_GROUPS):
            bm = xc_ref[rows, SSD_WIDTH + g * SSD_STATE:SSD_WIDTH + (g + 1) * SSD_STATE]
            cm = xc_ref[rows, SSD_WIDTH + (SSD_GROUPS + g) * SSD_STATE:
                        SSD_WIDTH + (SSD_GROUPS + g + 1) * SSD_STATE]
            cm_b = cm.astype(BF16)
            cb = _dot_nt(cm_b, bm.astype(BF16))
            bm_t = bm.T
            h_t = st_ref[g]
            y_off = _dot(cm_b, h_t.astype(BF16))
            new_state = []
            for jp in range(pairs_per_group):
                pair = g * pairs_per_group + jp
                lanes = slice(pair * LANES, (pair + 1) * LANES)
                sc, bsc, dec = [], [], []
                for h in (2 * pair, 2 * pair + 1):
                    c_col = _colb(cum, h)
                    seg = c_col - _rowb(cum_t, h, L)
                    lm = jnp.exp(jnp.where(causal, seg, -jnp.inf))
                    sc.append((cb * lm * _rowb(dt_t, h, L)).astype(BF16))
                    bsc.append((bm_t * _rowb(dtw_t, h, L)).astype(BF16))
                    dec.append(jnp.exp(c_col))
                xp = xs_b[:, lanes]
                zero = jnp.zeros_like(xp)
                rhs = jnp.concatenate([jnp.where(lo_half, xp, zero), jnp.where(lo_half, zero, xp)], axis=0)
                y_diag = _dot(jnp.concatenate(sc, axis=1), rhs)
                local = _dot(jnp.concatenate(bsc, axis=1), rhs)
                y_parts.append(y_diag + y_off[:, jp * LANES:(jp + 1) * LANES]
                               * jnp.where(lo_half, dec[0], dec[1]))
                new_state.append(h_t[:, jp * LANES:(jp + 1) * LANES] * dec_last_x[:, lanes] + local)
            st_ref[g] = jnp.concatenate(new_state, axis=1)
        y = jnp.concatenate(y_parts, axis=1) + xs * dskip_ref[...]
        y = y * _silu(z_ref[rows, :])
        outs = []
        for g in range(SSD_GROUPS):
            yg = y[:, g * gs:(g + 1) * gs]
            outs.append(yg * lax.rsqrt(jnp.mean(yg * yg, axis=-1, keepdims=True) + EPS))
        y_ref[rows, :] = (jnp.concatenate(outs, axis=1) * gnw_ref[...]).astype(BF16)
        return carry

    _for_chunks(tb // L, chunk)


def _ml_kernel(x_ref, nw_ref, w_ref, cw_ref, cb_ref, wqk_ref, wv_ref, wif_ref, bif_ref, lnw_ref,
               skip_ref, y_ref, pre_ref, mc_ref, oz_ref, qkv_ref, if_ref, ct_ref, nm_ref, m_ref):
    tb = x_ref.shape[0]
    L = CHUNK
    D = ML_HEAD_DIM

    @pl.when(pl.program_id(1) == 0)
    def _():
        pre_ref[:, 0:SUBLANES, :] = jnp.zeros((pre_ref.shape[0], SUBLANES, LANES), F32)
        ct_ref[...] = jnp.zeros(ct_ref.shape, F32)
        nm_ref[...] = jnp.zeros(nm_ref.shape, F32)
        m_ref[...] = jnp.zeros(m_ref.shape, F32)

    for r0 in range(0, tb, PROJ_ROWS):
        prow = pl.ds(r0, PROJ_ROWS)
        hn = _rmsnorm(x_ref[prow, :], nw_ref[...]).astype(BF16)
        _store_col_blocks(pre_ref, 0, SUBLANES + r0, _dot(hn, w_ref[:, 0:ML_WIDTH]))
        oz_ref[prow, :] = _dot(hn, w_ref[:, ML_WIDTH:])
        for h in range(ML_HEADS):
            v = _dot(pre_ref[h, pl.ds(SUBLANES + r0, PROJ_ROWS), :].astype(BF16), wv_ref[h])
            qkv_ref[prow, h * 3 * D + 2 * D:(h + 1) * 3 * D] = v
        _causal_conv_silu(pre_ref, cw_ref, cb_ref, mc_ref, r0, PROJ_ROWS)
        for h in range(ML_HEADS):
            qkv_ref[prow, h * 3 * D:h * 3 * D + 2 * D] = _dot(
                mc_ref[prow, h * D:(h + 1) * D].astype(BF16), wqk_ref[h])
        if_ref[prow, :] = _dot(qkv_ref[prow, :].astype(BF16), wif_ref[...]) + bif_ref[...]
    _conv_carry(pre_ref, tb)

    r_i = lax.broadcasted_iota(jnp.int32, (L, L), 0)
    c_i = lax.broadcasted_iota(jnp.int32, (L, L), 1)
    causal = c_i <= r_i
    tri = _tril_ones_bf16(L)
    scale = D ** -0.5

    ones_b = jnp.ones((L, D), BF16)
    heads = range(ML_HEADS)

    def chunk(c, carry):
        r0 = _chunk_start(c)
        rows = pl.ds(r0, L)
        if_pre = if_ref[rows, :]
        cum = _cumsum_time(_log_sigmoid(if_pre), tri)
        cum_t = cum.T
        if_t = if_pre.T
        q_b = [(qkv_ref[rows, h * 3 * D:h * 3 * D + D] * scale).astype(BF16) for h in heads]
        k = [qkv_ref[rows, h * 3 * D + D:h * 3 * D + 2 * D] for h in heads]
        v_aug = [jnp.concatenate([qkv_ref[rows, h * 3 * D + 2 * D:(h + 1) * 3 * D].astype(BF16), ones_b],
                                 axis=1) for h in heads]
        qk = [_dot_nt(q_b[h], k[h].astype(BF16)) for h in heads]
        k_t = [k[h].T for h in heads]
        m_in = [m_ref[h:h + 1, :] for h in heads]
        cn_in = [jnp.concatenate([ct_ref[h], nm_ref[h]], axis=1) for h in heads]
        c_col = [_colb(cum, ML_HEADS + h) for h in heads]
        a_row = [if_t[h:h + 1, :] - cum_t[ML_HEADS + h:ML_HEADS + h + 1, :] for h in heads]
        log_d = [jnp.where(causal, c_col[h] + a_row[h], -jnp.inf) for h in heads]
        log_inter = [c_col[h] + m_in[h] for h in heads]
        m_t = [jnp.maximum(log_inter[h], jnp.max(log_d[h], axis=1, keepdims=True)) for h in heads]
        s_b = [(qk[h] * jnp.exp(log_d[h] - m_t[h])).astype(BF16) for h in heads]
        w_inter = [jnp.exp(log_inter[h] - m_t[h]) for h in heads]
        intra = [_dot(s_b[h], v_aug[h]) for h in heads]
        inter = [_dot(q_b[h], cn_in[h].astype(BF16)) for h in heads]
        g_row = [c_col[h][L - 1:L, :] for h in heads]
        log_end = [g_row[h] + a_row[h] for h in heads]
        m_loc = [jnp.max(log_end[h], axis=1, keepdims=True) for h in heads]
        kw_b = [(k_t[h] * jnp.exp(log_end[h] - m_loc[h])).astype(BF16) for h in heads]
        local = [_dot(kw_b[h], v_aug[h]) for h in heads]
        for h in heads:
            m_new = jnp.maximum(g_row[h] + m_in[h], m_loc[h])
            a_old = jnp.exp(g_row[h] + m_in[h] - m_new)[:, 0:1]
            a_loc = jnp.exp(m_loc[h] - m_new)[:, 0:1]
            cn_new = a_old * cn_in[h] + a_loc * local[h]
            ct_ref[h] = cn_new[:, 0:D]
            nm_ref[h] = cn_new[:, D:2 * D]
            m_ref[h:h + 1, :] = m_new
        hh = []
        for h in heads:
            both = intra[h] + jnp.concatenate([w_inter[h], w_inter[h]], axis=1) * inter[h]
            den = jnp.maximum(jnp.abs(both[:, D:2 * D]), jnp.exp(-m_t[h]))
            hh.append(both[:, 0:D] / den)
        mu = [jnp.mean(hh[h], axis=-1, keepdims=True) for h in heads]
        xc = [hh[h] - mu[h] for h in heads]
        var = [jnp.mean(xc[h] * xc[h], axis=-1, keepdims=True) for h in heads]
        for h in heads:
            cols = slice(h * D, (h + 1) * D)
            ln = xc[h] * lax.rsqrt(var[h] + EPS) * lnw_ref[:, cols]
            o_gate = jax.nn.sigmoid(oz_ref[rows, h * D:(h + 1) * D])
            z = oz_ref[rows, ML_WIDTH + h * D:ML_WIDTH + (h + 1) * D]
            out = (ln * o_gate + skip_ref[:, cols] * mc_ref[rows, cols]) * _silu(z)
            y_ref[rows, cols] = out.astype(BF16)
        return carry

    _for_chunks(tb // L, chunk)


def _hg_level_table():
    l = np.arange(CHUNK)[:, None]
    s = np.arange(CHUNK)[None, :]
    x = l ^ s
    msb = np.floor(np.log2(np.maximum(x, 1))).astype(np.int32)
    return np.where(s < l, msb, np.where(s == l, HG_LEVELS, HG_LEVELS + 1)).astype(np.int32)


def _replicated_row(ref, blk, r):
    return ref[blk, pl.ds(r, SUBLANES, stride=0), :]


def _hg_reference_rows(cum_ref, level):
    b = 1 << level
    sub = lax.broadcasted_iota(jnp.int32, (SUBLANES, LANES), 0)
    col_blocks = []
    for blk in range(cum_ref.shape[0]):
        def row8(r, blk=blk):
            return _replicated_row(cum_ref, blk, r)
        pieces = []
        if 2 * b >= SUBLANES:
            for i in range(CHUNK // (2 * b)):
                pieces.extend([row8(i * 2 * b + b - 1)] * (2 * b // SUBLANES))
        else:
            for grp in range(CHUNK // SUBLANES):
                base = grp * SUBLANES
                acc = row8(base + b - 1)
                for i in range(1, SUBLANES // (2 * b)):
                    acc = jnp.where(sub >= i * 2 * b, row8(base + i * 2 * b + b - 1), acc)
                pieces.append(acc)
        col_blocks.append(jnp.concatenate(pieces, axis=0))
    return jnp.concatenate(col_blocks, axis=1)


def _hg_kernel(x_ref, nw_ref, w_ref, lb_ref, gnw_ref, lv_ref, y_ref, pj_ref, cum_ref, st_ref):
    tb = x_ref.shape[0]
    L = CHUNK
    D = HG_HEAD_DIM
    W = HG_WIDTH

    @pl.when(pl.program_id(1) == 0)
    def _():
        st_ref[...] = jnp.zeros(st_ref.shape, F32)

    hn = _rmsnorm(x_ref[...], nw_ref[...]).astype(BF16)
    pj_ref[...] = _dot(hn, w_ref[...])
    tri = _tril_ones_bf16(L)
    lb = lb_ref[...]
    lb_pos = lb > 0.0

    def chunk(c, carry):
        r0 = _chunk_start(c)
        rows = pl.ds(r0, L)
        fx = pj_ref[rows, W:2 * W]
        a = jnp.abs(fx)
        t = jnp.exp(-a)
        pos = fx >= 0.0
        one_t = 1.0 + t
        log_num = jnp.where(pos, jnp.log(1.0 + lb * t), jnp.where(lb_pos, jnp.log(t + lb), -a))
        log_f = log_num - jnp.log(one_t)
        k = (1.0 - lb) * jnp.where(pos, t, 1.0) / one_t
        q = _silu(pj_ref[rows, 0:W])
        v_b = pj_ref[rows, 2 * W:3 * W].astype(BF16)
        cum = _cumsum_time(log_f, tri)
        _store_col_blocks(cum_ref, 0, 0, cum)
        lv = lv_ref[...]
        q_b = q.astype(BF16)
        k_b = k.astype(BF16)
        attn = []
        for h in range(HG_HEADS):
            cols = slice(h * D, (h + 1) * D)
            attn.append(jnp.where(lv == HG_LEVELS, _dot_nt(q_b[:, cols], k_b[:, cols]), 0.0))
        for level in range(HG_LEVELS):
            e = jnp.exp(-jnp.abs(cum - _hg_reference_rows(cum_ref, level)))
            qe = (q * e).astype(BF16)
            ke = (k * e).astype(BF16)
            for h in range(HG_HEADS):
                cols = slice(h * D, (h + 1) * D)
                attn[h] = jnp.where(lv == level, _dot_nt(qe[:, cols], ke[:, cols]), attn[h])
        cum_last = cum[L - 1:L, :]
        q_dec = (q * jnp.exp(cum)).astype(BF16)
        k_end = k * jnp.exp(cum_last - cum)
        dec_last = jnp.exp(cum_last)
        gz = _silu(pj_ref[rows,3 * W:4 * W])
        for h in range(HG_HEADS):
            cols = slice(h * D, (h + 1) * D)
            st = st_ref[h]
            o = _dot(attn[h].astype(BF16), v_b[:, cols]) + _dot_nt(q_dec[:, cols], st.astype(BF16))
            v_t = pj_ref[rows,2 * W + h * D:2 * W + (h + 1) * D].T.astype(BF16)
            st_ref[h] = st * dec_last[:, cols] + _dot(v_t, k_end[:, cols].astype(BF16))
            on = o * lax.rsqrt(jnp.mean(o * o, axis=-1, keepdims=True) + EPS)
            y_ref[rows, cols] = (on * gnw_ref[:, cols] * gz[:, cols]).astype(BF16)
        return carry

    _for_chunks(tb // L, chunk)


def _merge_kernel(x_ref, nw_ref, wg_ref, ys_ref, ym_ref, yh_ref, wbs_ref, wbm_ref, wbh_ref, wo_ref,
                  fnw_ref, o_ref, *, final_norm):
    x = x_ref[...]
    hn = _rmsnorm(x, nw_ref[...]).astype(BF16)
    d = x.shape[-1]
    merged = jax.nn.sigmoid(_dot(hn, wg_ref[:, 0:d])) * _dot(ys_ref[...], wbs_ref[...])
    merged = merged + jax.nn.sigmoid(_dot(hn, wg_ref[:, d:2 * d])) * _dot(ym_ref[...], wbm_ref[...])
    merged = merged + jax.nn.sigmoid(_dot(hn, wg_ref[:, 2 * d:3 * d])) * _dot(yh_ref[...], wbh_ref[...])
    out = x + _dot(merged.astype(BF16), wo_ref[...])
    if final_norm:
        out = _rmsnorm(out, fnw_ref[...])
    o_ref[...] = out


def _const_spec(shape):
    nd = len(shape)
    return pl.BlockSpec(shape, lambda b, t: (0,) * nd)


def _tok_spec(tb, width):
    return pl.BlockSpec((None, tb, width), lambda b, t: (b, t, 0))


def _layer_spec(block_shape, layer, col_block=0):
    nd = len(block_shape)
    return pl.BlockSpec((None,) + tuple(block_shape),
                        lambda b, t: (layer,) + (0,) * (nd - 1) + (col_block,))


def _with_specs(consts):
    pairs = [c if isinstance(c, tuple) else (c, _const_spec(c.shape)) for c in consts]
    return [a for a, _ in pairs], [sp for _, sp in pairs]


def _mixer_call(kernel_fn, name, x, consts, out_width, scratch_shapes, tb):
    bsz, seq, d = x.shape
    consts, const_specs = _with_specs(consts)
    return pl.pallas_call(
        kernel_fn,
        grid=(bsz, seq // tb),
        in_specs=[_tok_spec(tb, d)] + const_specs,
        out_specs=_tok_spec(tb, out_width),
        out_shape=jax.ShapeDtypeStruct((bsz, seq, out_width), BF16),
        scratch_shapes=scratch_shapes,
        compiler_params=pltpu.CompilerParams(
            dimension_semantics=("parallel", "arbitrary"),
            vmem_limit_bytes=VMEM_LIMIT_BYTES),
        name=name,
    )(x, *consts)


def _row(v):
    return v.reshape(1, -1).astype(F32)


_W_SEGMENTS = (('gate', 0, 3 * 1024), ('hg', 2, 4 * HG_WIDTH), ('ml', 4, 3 * ML_WIDTH),
               ('ssd', 3, SSD_CONV_DIM + LANES + SSD_WIDTH))


def _permuted_w_in(w_in):
    depth, d, _ = w_in.shape
    o_dt = SSD_CONV_DIM
    o_z = o_dt + SSD_HEADS
    o_ml = o_z + SSD_WIDTH
    o_hg = o_ml + 3 * ML_WIDTH
    o_gate = o_hg + 4 * HG_WIDTH
    zeros = lambda n: jnp.zeros((depth, d, n), w_in.dtype)
    segs = dict(gate=w_in[..., o_gate:], hg=w_in[..., o_hg:o_gate], ml=w_in[..., o_ml:o_hg],
                ssd=jnp.concatenate([w_in[..., 0:o_dt], w_in[..., o_dt:o_z], zeros(LANES - SSD_HEADS),
                                     w_in[..., o_z:o_ml]], axis=-1))
    parts, pos = [], 0
    for name, blk, width in sorted(_W_SEGMENTS, key=lambda sg: sg[1] * sg[2]):
        assert segs[name].shape[-1] == width and blk * width >= pos
        if blk * width > pos:
            parts.append(zeros(blk * width - pos))
        parts.append(segs[name])
        pos = (blk + 1) * width
    return jnp.concatenate(parts, axis=-1).astype(BF16)


def _w_in_window(w_perm, name, layer):
    _, blk, width = next(sg for sg in _W_SEGMENTS if sg[0] == name)
    return (w_perm, _layer_spec((w_perm.shape[1], width), layer, blk))


def _layer(x, layer, lb, p, big, final_norm_w, final_norm):
    bsz, seq, d = x.shape
    tb = min(TOKEN_BLOCK, seq)
    assert seq % tb == 0 and tb % CHUNK == 0
    w_ssd = _w_in_window(big['w_in'], 'ssd', layer)
    w_ml = _w_in_window(big['w_in'], 'ml', layer)
    w_hg = _w_in_window(big['w_in'], 'hg', layer)
    w_gate = _w_in_window(big['w_in'], 'gate', layer)
    nw = _row(p['norm_w'])

    pad_h = (0, LANES - SSD_HEADS)
    ssd_consts = [
        nw, w_ssd, p['ssd_conv_w'].astype(F32), _row(p['ssd_conv_b']),
        _row(jnp.pad(p['ssd_dt_bias'], pad_h)), _row(jnp.pad(p['ssd_a_log'], pad_h)),
        _row(jnp.repeat(p['ssd_d'], SSD_HEAD_DIM)), _row(p['ssd_norm_w'])]
    y_ssd = _mixer_call(
        _ssd_kernel, 'ssd_mixer', x, ssd_consts, SSD_WIDTH,
        [pltpu.VMEM((SSD_CONV_DIM // LANES, tb + SUBLANES, LANES), F32),
         pltpu.VMEM((tb, SSD_CONV_DIM), F32),
         pltpu.VMEM((tb, LANES), F32), pltpu.VMEM((tb, SSD_WIDTH), F32),
         pltpu.VMEM((SSD_GROUPS, SSD_STATE, SSD_WIDTH // SSD_GROUPS), F32)], tb)

    wqk = jnp.concatenate([p['ml_wq'], p['ml_wk']], axis=-1).astype(BF16)
    wv = p['ml_wv'].astype(BF16)
    wif = jnp.pad(p['ml_w_if'], ((0, 0), (0, LANES - 2 * ML_HEADS))).astype(BF16)
    bif = _row(jnp.pad(p['ml_b_if'], (0, LANES - 2 * ML_HEADS)))
    ml_consts = [nw, w_ml, p['ml_conv_w'].astype(F32), _row(p['ml_conv_b']), wqk, wv, wif, bif,
                 _row(p['ml_norm_w']), _row(p['ml_skip'])]
    y_ml = _mixer_call(
        _ml_kernel, 'mlstm_mixer', x, ml_consts, ML_WIDTH,
        [pltpu.VMEM((ML_WIDTH // LANES, tb + SUBLANES, LANES), F32), pltpu.VMEM((tb, ML_WIDTH), F32),
         pltpu.VMEM((tb, 2 * ML_WIDTH), F32), pltpu.VMEM((tb, 3 * ML_WIDTH), F32),
         pltpu.VMEM((tb, LANES), F32),
         pltpu.VMEM((ML_HEADS, ML_HEAD_DIM, ML_HEAD_DIM), F32),
         pltpu.VMEM((ML_HEADS, ML_HEAD_DIM, ML_HEAD_DIM), F32), pltpu.VMEM((SUBLANES, LANES), F32)], tb)

    hg_consts = [nw, w_hg, _row(lb), _row(p['hg_norm_w']), jnp.asarray(_hg_level_table())]
    y_hg = _mixer_call(
        _hg_kernel, 'hgrn2_mixer', x, hg_consts, HG_WIDTH,
        [pltpu.VMEM((tb, 4 * HG_WIDTH), F32), pltpu.VMEM((HG_WIDTH // LANES, CHUNK, LANES), F32),
         pltpu.VMEM((HG_HEADS, HG_HEAD_DIM, HG_HEAD_DIM), F32)], tb)

    merge_consts_a, merge_specs_a = _with_specs([nw, w_gate])
    merge_consts_b, merge_specs_b = _with_specs(
        [(big[k], _layer_spec(big[k].shape[1:], layer))
         for k in ('w_branch_ssd', 'w_branch_ml', 'w_branch_hg', 'w_out')] + [_row(final_norm_w)])
    return pl.pallas_call(
        functools.partial(_merge_kernel, final_norm=final_norm),
        grid=(bsz, seq // tb),
        in_specs=([_tok_spec(tb, d)] + merge_specs_a
                  + [_tok_spec(tb, SSD_WIDTH), _tok_spec(tb, ML_WIDTH), _tok_spec(tb, HG_WIDTH)]
                  + merge_specs_b),
        out_specs=_tok_spec(tb, d),
        out_shape=jax.ShapeDtypeStruct((bsz, seq, d), F32),
        compiler_params=pltpu.CompilerParams(
            dimension_semantics=("parallel", "parallel"),
            vmem_limit_bytes=VMEM_LIMIT_BYTES),
        name='merge_out',
    )(x, *merge_consts_a, y_ssd, y_ml, y_hg, *merge_consts_b)


_LAYER_PARAMS = ('norm_w', 'ssd_conv_w', 'ssd_conv_b', 'ssd_dt_bias', 'ssd_a_log', 'ssd_d',
                 'ssd_norm_w', 'ml_conv_w', 'ml_conv_b', 'ml_wq', 'ml_wk', 'ml_wv', 'ml_w_if',
                 'ml_b_if', 'ml_norm_w', 'ml_skip', 'hg_norm_w')


def kernel(x, norm_w, w_in, ssd_conv_w, ssd_conv_b, ssd_dt_bias, ssd_a_log, ssd_d, ssd_norm_w, ml_conv_w, ml_conv_b, ml_wq, ml_wk, ml_wv, ml_w_if, ml_b_if, ml_norm_w, ml_skip, hg_lower_bounds, hg_norm_w, w_branch_ssd, w_branch_ml, w_branch_hg, w_out, final_norm_w):
    stacked = dict(norm_w=norm_w, ssd_conv_w=ssd_conv_w, ssd_conv_b=ssd_conv_b,
                   ssd_dt_bias=ssd_dt_bias, ssd_a_log=ssd_a_log, ssd_d=ssd_d, ssd_norm_w=ssd_norm_w,
                   ml_conv_w=ml_conv_w, ml_conv_b=ml_conv_b, ml_wq=ml_wq, ml_wk=ml_wk, ml_wv=ml_wv,
                   ml_w_if=ml_w_if, ml_b_if=ml_b_if, ml_norm_w=ml_norm_w, ml_skip=ml_skip,
                   hg_norm_w=hg_norm_w)
    big = dict(w_in=_permuted_w_in(w_in), w_branch_ssd=w_branch_ssd.astype(BF16),
               w_branch_ml=w_branch_ml.astype(BF16), w_branch_hg=w_branch_hg.astype(BF16),
               w_out=w_out.astype(BF16))
    depth = norm_w.shape[0]
    lbs = jnp.cumsum(jax.nn.softmax(hg_lower_bounds.astype(F32), axis=0), axis=0)
    lbs = lbs - lbs[0]
    for l in range(depth):
        p = {k: stacked[k][l] for k in _LAYER_PARAMS}
        x = _layer(x, l, lbs[l], p, big, final_norm_w, final_norm=(l == depth - 1))
    return x
```

```python
import functools

import numpy as np
import jax
import jax.numpy as jnp
from jax import lax
from jax.experimental import pallas as pl
from jax.experimental.pallas import tpu as pltpu

F32 = jnp.float32
BF16 = jnp.bfloat16

EPS = 1e-6
LOG2E = 1.4426950408889634
CONV_K = 4
LANES = 128
SUBLANES = 8
CHUNK = 128
TOKEN_BLOCK = 512
PROJ_ROWS = 256
VMEM_LIMIT_BYTES = 56 * 1024 * 1024

SSD_HEAD_DIM = 64
SSD_HEADS = 16
SSD_GROUPS = 2
SSD_STATE = 128
SSD_WIDTH = SSD_HEADS * SSD_HEAD_DIM
SSD_CONV_DIM = SSD_WIDTH + 2 * SSD_GROUPS * SSD_STATE
ML_HEADS = 4
ML_HEAD_DIM = 128
ML_WIDTH = ML_HEADS * ML_HEAD_DIM
HG_HEADS = 4
HG_HEAD_DIM = 128
HG_WIDTH = HG_HEADS * HG_HEAD_DIM
HG_LEVELS = 7


def _dot(a, b):
    return jnp.dot(a, b, preferred_element_type=F32)


def _dot_nt(a, b):
    return lax.dot_general(a, b, (((1,), (1,)), ((), ())), preferred_element_type=F32)


def _rmsnorm(x, w):
    return x * lax.rsqrt(jnp.mean(x * x, axis=-1, keepdims=True) + EPS) * w


def _softplus(x):
    return jnp.maximum(x, 0.0) + jnp.log1p(jnp.exp(-jnp.abs(x)))


def _log_sigmoid(x):
    return jnp.minimum(x, 0.0) - jnp.log1p(jnp.exp(-jnp.abs(x)))


def _silu(x):
    return x * jax.nn.sigmoid(x)


def _tril_ones_bf16(n):
    r = lax.broadcasted_iota(jnp.int32, (n, n), 0)
    c = lax.broadcasted_iota(jnp.int32, (n, n), 1)
    return jnp.where(c <= r, 1.0, 0.0).astype(BF16)


def _cumsum_time(x, tri):
    hi = x.astype(BF16)
    r1 = x - hi.astype(F32)
    mid = r1.astype(BF16)
    lo = (r1 - mid.astype(F32)).astype(BF16)
    return _dot(tri, hi) + _dot(tri, mid) + _dot(tri, lo)


def _colb(x, j, n=LANES):
    return jnp.broadcast_to(x[:, j:j + 1], (x.shape[0], n))


def _rowb(x, j, m):
    return jnp.broadcast_to(x[j:j + 1, :], (m, x.shape[1]))


def _expand_heads(v):
    rows = v.shape[0]
    lane = lax.broadcasted_iota(jnp.int32, (rows, LANES), 1)
    parts = []
    for j in range(SSD_HEADS // 2):
        a = _colb(v, 2 * j)
        b = _colb(v, 2 * j + 1)
        parts.append(jnp.where(lane < SSD_HEAD_DIM, a, b))
    return jnp.concatenate(parts, axis=1)


def _chunk_start(c):
    return c * CHUNK if isinstance(c, int) else pl.multiple_of(c * CHUNK, CHUNK)


def _for_chunks(n, body):
    for c in range(n):
        body(c, 0)


def _store_col_blocks(dst_ref, first_block, row0, val):
    for i in range(val.shape[1] // LANES):
        dst_ref[first_block + i, pl.ds(row0, val.shape[0]), :] = val[:, i * LANES:(i + 1) * LANES]


def _causal_conv_silu(pre_ref, cw_ref, cb_ref, out_ref, r0, n):
    for blk in range(pre_ref.shape[0]):
        cols = slice(blk * LANES, (blk + 1) * LANES)
        acc = cb_ref[:, cols] + cw_ref[0:1, cols] * pre_ref[blk, pl.ds(r0 + SUBLANES - 3, n), :]
        for j in range(1, CONV_K):
            acc = acc + cw_ref[j:j + 1, cols] * pre_ref[blk, pl.ds(r0 + SUBLANES - 3 + j, n), :]
        out_ref[pl.ds(r0, n), cols] = _silu(acc)


def _conv_carry(pre_ref, tb):
    for blk in range(pre_ref.shape[0]):
        pre_ref[blk, 0:SUBLANES, :] = pre_ref[blk, pl.ds(tb, SUBLANES), :]


def _ssd_kernel(hn_ref, w_ref, cw_ref, cb_ref, dtb_ref, alog_ref, dskip_ref, gnw_ref,
                y_ref, pre_ref, xc_ref, dt_ref, z_ref, st_ref):
    tb = hn_ref.shape[0]
    L = CHUNK

    @pl.when(pl.program_id(1) == 0)
    def _():
        pre_ref[:, 0:SUBLANES, :] = jnp.zeros((pre_ref.shape[0], SUBLANES, LANES), F32)
        st_ref[...] = jnp.zeros(st_ref.shape, F32)

    group = 4 * LANES
    for r0 in range(0, tb, PROJ_ROWS):
        prow = pl.ds(r0, PROJ_ROWS)
        hn = hn_ref[prow, :]
        for c0 in range(0, SSD_CONV_DIM, group):
            _store_col_blocks(pre_ref, c0 // LANES, SUBLANES + r0, _dot_nt(hn, w_ref[c0:c0 + group, :]))
        dt_ref[prow, :] = _dot_nt(hn, w_ref[SSD_CONV_DIM:SSD_CONV_DIM + LANES, :])
        z_ref[prow, :] = _dot_nt(hn, w_ref[SSD_CONV_DIM + LANES:, :])
        _causal_conv_silu(pre_ref, cw_ref, cb_ref, xc_ref, r0, PROJ_ROWS)
    _conv_carry(pre_ref, tb)

    lane_row = lax.broadcasted_iota(jnp.int32, (1, LANES), 1)
    a_row = jnp.where(lane_row < SSD_HEADS, -jnp.exp(alog_ref[...]) * LOG2E, 0.0)
    r_i = lax.broadcasted_iota(jnp.int32, (L, L), 0)
    c_i = lax.broadcasted_iota(jnp.int32, (L, L), 1)
    causal = c_i <= r_i
    lane = lax.broadcasted_iota(jnp.int32, (L, LANES), 1)
    lo_half = lane < SSD_HEAD_DIM
    tri = _tril_ones_bf16(L)
    gs = SSD_WIDTH // SSD_GROUPS
    pairs_per_group = SSD_HEADS // SSD_GROUPS // 2

    def chunk(c, carry):
        r0 = _chunk_start(c)
        rows = pl.ds(r0, L)
        xs = xc_ref[rows, 0:SSD_WIDTH]
        dt = _softplus(dt_ref[rows, :] + dtb_ref[...])
        cum = _cumsum_time(dt * a_row, tri)
        cum_last = cum[L - 1:L, :]
        r_t = (cum - jnp.log(dt) * LOG2E).T
        dtw_t = (dt * jnp.exp2(cum_last - cum)).T
        dec_last_x = _expand_heads(jnp.exp2(cum_last))
        xs_b = xs.astype(BF16)

        y_parts = []
        for g in range(SSD_GROUPS):
            bm = xc_ref[rows, SSD_WIDTH + g * SSD_STATE:SSD_WIDTH + (g + 1) * SSD_STATE]
            cm = xc_ref[rows, SSD_WIDTH + (SSD_GROUPS + g) * SSD_STATE:
                        SSD_WIDTH + (SSD_GROUPS + g + 1) * SSD_STATE]
            cm_b = cm.astype(BF16)
            cb = _dot_nt(cm_b, bm.astype(BF16))
            bm_t = bm.T
            h_t = st_ref[g]
            y_off = _dot(cm_b, h_t.astype(BF16))
            new_state = []
            for jp in range(pairs_per_group):
                pair = g * pairs_per_group + jp
                lanes = slice(pair * LANES, (pair + 1) * LANES)
                sc, bsc, dec = [], [], []
                for h in (2 * pair, 2 * pair + 1):
                    c_col = _colb(cum, h)
                    seg = c_col - _rowb(r_t, h, L)
                    sc.append((cb * jnp.exp2(jnp.where(causal, seg, -jnp.inf))).astype(BF16))
                    bsc.append((bm_t * _rowb(dtw_t, h, L)).astype(BF16))
                    dec.append(jnp.exp2(c_col))
                xp = xs_b[:, lanes]
                zero = jnp.zeros_like(xp)
                rhs = jnp.concatenate([jnp.where(lo_half, xp, zero), jnp.where(lo_half, zero, xp)], axis=0)
                y_diag = _dot(jnp.concatenate(sc, axis=1), rhs)
                local = _dot(jnp.concatenate(bsc, axis=1), rhs)
                y_parts.append(y_diag + y_off[:, jp * LANES:(jp + 1) * LANES]
                               * jnp.where(lo_half, dec[0], dec[1]))
                new_state.append(h_t[:, jp * LANES:(jp + 1) * LANES] * dec_last_x[:, lanes] + local)
            st_ref[g] = jnp.concatenate(new_state, axis=1)
        y = jnp.concatenate(y_parts, axis=1) + xs * dskip_ref[...]
        y = y * _silu(z_ref[rows, :])
        outs = []
        for g in range(SSD_GROUPS):
            yg = y[:, g * gs:(g + 1) * gs]
            outs.append(yg * lax.rsqrt(jnp.mean(yg * yg, axis=-1, keepdims=True) + EPS))
        y_ref[rows, :] = (jnp.concatenate(outs, axis=1) * gnw_ref[...]).astype(BF16)
        return carry

    _for_chunks(tb // L, chunk)


def _ml_kernel(hn_ref, w_ref, cw_ref, cb_ref, wqk_ref, wv_ref, wif_ref, bif_ref, lnw_ref,
               skip_ref, y_ref, pre_ref, mc_ref, oz_ref, qkv_ref, if_ref, ct_ref, nm_ref, m_ref):
    tb = hn_ref.shape[0]
    L = CHUNK
    D = ML_HEAD_DIM

    @pl.when(pl.program_id(1) == 0)
    def _():
        pre_ref[:, 0:SUBLANES, :] = jnp.zeros((pre_ref.shape[0], SUBLANES, LANES), F32)
        ct_ref[...] = jnp.zeros(ct_ref.shape, F32)
        nm_ref[...] = jnp.zeros(nm_ref.shape, F32)
        m_ref[...] = jnp.zeros(m_ref.shape, F32)

    for r0 in range(0, tb, PROJ_ROWS):
        prow = pl.ds(r0, PROJ_ROWS)
        hn = hn_ref[prow, :]
        _store_col_blocks(pre_ref, 0, SUBLANES + r0, _dot_nt(hn, w_ref[0:ML_WIDTH, :]))
        oz_ref[prow, :] = _dot_nt(hn, w_ref[ML_WIDTH:, :])
        for h in range(ML_HEADS):
            v = _dot(pre_ref[h, pl.ds(SUBLANES + r0, PROJ_ROWS), :].astype(BF16), wv_ref[h])
            qkv_ref[prow, h * 3 * D + 2 * D:(h + 1) * 3 * D] = v
        _causal_conv_silu(pre_ref, cw_ref, cb_ref, mc_ref, r0, PROJ_ROWS)
        for h in range(ML_HEADS):
            qkv_ref[prow, h * 3 * D:h * 3 * D + 2 * D] = _dot(
                mc_ref[prow, h * D:(h + 1) * D].astype(BF16), wqk_ref[h])
        if_ref[prow, :] = _dot(qkv_ref[prow, :].astype(BF16), wif_ref[...]) + bif_ref[...]
    _conv_carry(pre_ref, tb)

    r_i = lax.broadcasted_iota(jnp.int32, (L, L), 0)
    c_i = lax.broadcasted_iota(jnp.int32, (L, L), 1)
    causal = c_i <= r_i
    tri = _tril_ones_bf16(L)
    scale = D ** -0.5

    ones_b = jnp.ones((L, D), BF16)
    heads = range(ML_HEADS)

    def chunk(c, carry):
        r0 = _chunk_start(c)
        rows = pl.ds(r0, L)
        if_pre = if_ref[rows, :]
        cum = _cumsum_time(_log_sigmoid(if_pre) * LOG2E, tri)
        cum_t = cum.T
        if_t = (if_pre * LOG2E).T
        q_b = [(qkv_ref[rows, h * 3 * D:h * 3 * D + D] * scale).astype(BF16) for h in heads]
        k = [qkv_ref[rows, h * 3 * D + D:h * 3 * D + 2 * D] for h in heads]
        v_aug = [jnp.concatenate([qkv_ref[rows, h * 3 * D + 2 * D:(h + 1) * 3 * D].astype(BF16), ones_b],
                                 axis=1) for h in heads]
        qk = [_dot_nt(q_b[h], k[h].astype(BF16)) for h in heads]
        k_t = [k[h].T for h in heads]
        m_in = [m_ref[h:h + 1, :] for h in heads]
        cn_in = [jnp.concatenate([ct_ref[h], nm_ref[h]], axis=1) for h in heads]
        c_col = [_colb(cum, ML_HEADS + h) for h in heads]
        a_row = [if_t[h:h + 1, :] - cum_t[ML_HEADS + h:ML_HEADS + h + 1, :] for h in heads]
        log_d = [jnp.where(causal, c_col[h] + a_row[h], -jnp.inf) for h in heads]
        log_inter = [c_col[h] + m_in[h] for h in heads]
        m_t = [jnp.maximum(log_inter[h], jnp.max(log_d[h], axis=1, keepdims=True)) for h in heads]
        s_b = [(qk[h] * jnp.exp2(log_d[h] - m_t[h])).astype(BF16) for h in heads]
        w_inter = [jnp.exp2(log_inter[h] - m_t[h]) for h in heads]
        intra = [_dot(s_b[h], v_aug[h]) for h in heads]
        inter = [_dot(q_b[h], cn_in[h].astype(BF16)) for h in heads]
        g_row = [c_col[h][L - 1:L, :] for h in heads]
        log_end = [g_row[h] + a_row[h] for h in heads]
        m_loc = [jnp.max(log_end[h], axis=1, keepdims=True) for h in heads]
        kw_b = [(k_t[h] * jnp.exp2(log_end[h] - m_loc[h])).astype(BF16) for h in heads]
        local = [_dot(kw_b[h], v_aug[h]) for h in heads]
        for h in heads:
            m_new = jnp.maximum(g_row[h] + m_in[h], m_loc[h])
            a_old = jnp.exp2(g_row[h] + m_in[h] - m_new)[:, 0:1]
            a_loc = jnp.exp2(m_loc[h] - m_new)[:, 0:1]
            cn_new = a_old * cn_in[h] + a_loc * local[h]
            ct_ref[h] = cn_new[:, 0:D]
            nm_ref[h] = cn_new[:, D:2 * D]
            m_ref[h:h + 1, :] = m_new
        hh = []
        for h in heads:
            both = intra[h] + jnp.concatenate([w_inter[h], w_inter[h]], axis=1) * inter[h]
            den = jnp.maximum(jnp.abs(both[:, D:2 * D]), jnp.exp2(-m_t[h]))
            hh.append(both[:, 0:D] / den)
        mu = [jnp.mean(hh[h], axis=-1, keepdims=True) for h in heads]
        xc = [hh[h] - mu[h] for h in heads]
        var = [jnp.mean(xc[h] * xc[h], axis=-1, keepdims=True) for h in heads]
        for h in heads:
            cols = slice(h * D, (h + 1) * D)
            ln = xc[h] * lax.rsqrt(var[h] + EPS) * lnw_ref[:, cols]
            o_gate = jax.nn.sigmoid(oz_ref[rows, h * D:(h + 1) * D])
            z = oz_ref[rows, ML_WIDTH + h * D:ML_WIDTH + (h + 1) * D]
            out = (ln * o_gate + skip_ref[:, cols] * mc_ref[rows, cols]) * _silu(z)
            y_ref[rows, cols] = out.astype(BF16)
        return carry

    _for_chunks(tb // L, chunk)


def _hg_level_table():
    l = np.arange(CHUNK)[:, None]
    s = np.arange(CHUNK)[None, :]
    x = l ^ s
    msb = np.floor(np.log2(np.maximum(x, 1))).astype(np.int32)
    return np.where(s < l, msb, np.where(s == l, HG_LEVELS, HG_LEVELS + 1)).astype(np.int32)


def _replicated_row(ref, blk, r):
    return ref[blk, pl.ds(r, SUBLANES, stride=0), :]


def _hg_reference_rows(cum_ref, level):
    b = 1 << level
    sub = lax.broadcasted_iota(jnp.int32, (SUBLANES, LANES), 0)
    col_blocks = []
    for blk in range(cum_ref.shape[0]):
        def row8(r, blk=blk):
            return _replicated_row(cum_ref, blk, r)
        pieces = []
        if 2 * b >= SUBLANES:
            for i in range(CHUNK // (2 * b)):
                pieces.extend([row8(i * 2 * b + b - 1)] * (2 * b // SUBLANES))
        else:
            for grp in range(CHUNK // SUBLANES):
                base = grp * SUBLANES
                acc = row8(base + b - 1)
                for i in range(1, SUBLANES // (2 * b)):
                    acc = jnp.where(sub >= i * 2 * b, row8(base + i * 2 * b + b - 1), acc)
                pieces.append(acc)
        col_blocks.append(jnp.concatenate(pieces, axis=0))
    return jnp.concatenate(col_blocks, axis=1)


def _hg_kernel(hn_ref, w_ref, lb_ref, gnw_ref, lv_ref, y_ref, pj_ref, cum_ref, st_ref):
    tb = hn_ref.shape[0]
    L = CHUNK
    D = HG_HEAD_DIM
    W = HG_WIDTH

    @pl.when(pl.program_id(1) == 0)
    def _():
        st_ref[...] = jnp.zeros(st_ref.shape, F32)

    pj_ref[...] = _dot_nt(hn_ref[...], w_ref[...])
    tri = _tril_ones_bf16(L)
    lb = lb_ref[...]
    lb_pos = lb > 0.0

    def chunk(c, carry):
        r0 = _chunk_start(c)
        rows = pl.ds(r0, L)
        fx = pj_ref[rows, W:2 * W]
        a = jnp.abs(fx)
        t = jnp.exp(-a)
        pos = fx >= 0.0
        one_t = 1.0 + t
        log_num = jnp.where(pos, jnp.log(1.0 + lb * t), jnp.where(lb_pos, jnp.log(t + lb), -a))
        log2_f = (log_num - jnp.log(one_t)) * LOG2E
        k = (1.0 - lb) * jnp.where(pos, t, 1.0) / one_t
        q = _silu(pj_ref[rows, 0:W])
        v_b = pj_ref[rows, 2 * W:3 * W].astype(BF16)
        cum = _cumsum_time(log2_f, tri)
        _store_col_blocks(cum_ref, 0, 0, cum)
        lv = lv_ref[...]
        q_b = q.astype(BF16)
        k_b = k.astype(BF16)
        attn = []
        for h in range(HG_HEADS):
            cols = slice(h * D, (h + 1) * D)
            attn.append(jnp.where(lv == HG_LEVELS, _dot_nt(q_b[:, cols], k_b[:, cols]), 0.0))
        for level in range(HG_LEVELS):
            e = jnp.exp2(-jnp.abs(cum - _hg_reference_rows(cum_ref, level)))
            qe = (q * e).astype(BF16)
            ke = (k * e).astype(BF16)
            for h in range(HG_HEADS):
                cols = slice(h * D, (h + 1) * D)
                attn[h] = jnp.where(lv == level, _dot_nt(qe[:, cols], ke[:, cols]), attn[h])
        cum_last = cum[L - 1:L, :]
        q_dec = (q * jnp.exp2(cum)).astype(BF16)
        k_end = k * jnp.exp2(cum_last - cum)
        dec_last = jnp.exp2(cum_last)
        gz = _silu(pj_ref[rows,3 * W:4 * W])
        for h in range(HG_HEADS):
            cols = slice(h * D, (h + 1) * D)
            st = st_ref[h]
            o = _dot(attn[h].astype(BF16), v_b[:, cols]) + _dot_nt(q_dec[:, cols], st.astype(BF16))
            v_t = pj_ref[rows,2 * W + h * D:2 * W + (h + 1) * D].T.astype(BF16)
            st_ref[h] = st * dec_last[:, cols] + _dot(v_t, k_end[:, cols].astype(BF16))
            on = o * lax.rsqrt(jnp.mean(o * o, axis=-1, keepdims=True) + EPS)
            y_ref[rows, cols] = (on * gnw_ref[:, cols] * gz[:, cols]).astype(BF16)
        return carry

    _for_chunks(tb // L, chunk)


def _merge_kernel(x_ref, hn_ref, wg_ref, ys_ref, ym_ref, yh_ref, wbs_ref, wbm_ref, wbh_ref, wo_ref,
                  nnw_ref, o_ref, *hn_next_ref, last_layer):
    x = x_ref[...]
    hn = hn_ref[...]
    d = x.shape[-1]
    merged = jax.nn.sigmoid(_dot_nt(hn, wg_ref[0:d, :])) * _dot(ys_ref[...], wbs_ref[...])
    merged = merged + jax.nn.sigmoid(_dot_nt(hn, wg_ref[d:2 * d, :])) * _dot(ym_ref[...], wbm_ref[...])
    merged = merged + jax.nn.sigmoid(_dot_nt(hn, wg_ref[2 * d:3 * d, :])) * _dot(yh_ref[...], wbh_ref[...])
    out = x + _dot(merged.astype(BF16), wo_ref[...])
    if last_layer:
        o_ref[...] = _rmsnorm(out, nnw_ref[...])
    else:
        o_ref[...] = out
        hn_next_ref[0][...] = _rmsnorm(out, nnw_ref[...]).astype(BF16)


def _prenorm_kernel(x_ref, nw_ref, hn_ref):
    hn_ref[...] = _rmsnorm(x_ref[...], nw_ref[...]).astype(BF16)


def _segment_cast_kernel(w_ref, o_ref, *, dt_block):
    w = w_ref[0]
    if dt_block is not None:
        row = lax.broadcasted_iota(jnp.int32, w.shape, 0)
        keep = jnp.logical_or(pl.program_id(1) != dt_block, row < SSD_HEADS)
        w = jnp.where(keep, w, 0.0)
    o_ref[...] = w.astype(BF16)


def _const_spec(shape):
    nd = len(shape)
    return pl.BlockSpec(shape, lambda b, t: (0,) * nd)


def _tok_spec(tb, width):
    return pl.BlockSpec((None, tb, width), lambda b, t: (b, t, 0))


def _layer_spec(block_shape, layer, col_block=0):
    nd = len(block_shape)
    return pl.BlockSpec((None,) + tuple(block_shape),
                        lambda b, t: (layer,) + (0,) * (nd - 1) + (col_block,))


def _with_specs(consts):
    pairs = [c if isinstance(c, tuple) else (c, _const_spec(c.shape)) for c in consts]
    return [a for a, _ in pairs], [sp for _, sp in pairs]


def _mixer_call(kernel_fn, name, x, consts, out_width, scratch_shapes, tb):
    bsz, seq, d = x.shape
    consts, const_specs = _with_specs(consts)
    return pl.pallas_call(
        kernel_fn,
        grid=(bsz, seq // tb),
        in_specs=[_tok_spec(tb, d)] + const_specs,
        out_specs=_tok_spec(tb, out_width),
        out_shape=jax.ShapeDtypeStruct((bsz, seq, out_width), BF16),
        scratch_shapes=scratch_shapes,
        compiler_params=pltpu.CompilerParams(
            dimension_semantics=("parallel", "arbitrary"),
            vmem_limit_bytes=VMEM_LIMIT_BYTES),
        name=name,
    )(x, *consts)


def _row(v):
    return v.reshape(1, -1).astype(F32)


_SSD_COLS = SSD_CONV_DIM + SSD_HEADS + SSD_WIDTH
SEGMENT_ROWS = 512


def _segment_cast(w_t, src_row, n_blocks, rb, dt_block=None):
    depth, _, d = w_t.shape
    return pl.pallas_call(
        functools.partial(_segment_cast_kernel, dt_block=dt_block),
        grid=(depth, n_blocks),
        in_specs=[pl.BlockSpec((pl.Element(1), pl.Element(rb), pl.Element(d)),
                               lambda l, j: (l, pl.multiple_of(src_row(j), 16), 0))],
        out_specs=pl.BlockSpec((None, rb, d), lambda l, j: (l, j, 0)),
        out_shape=jax.ShapeDtypeStruct((depth, n_blocks * rb, d), BF16),
        compiler_params=pltpu.CompilerParams(
            dimension_semantics=("parallel", "parallel"), vmem_limit_bytes=VMEM_LIMIT_BYTES),
        name='w_in_segment_cast',
    )(w_t)


def _split_w_in(w_in):
    w_t = jnp.swapaxes(w_in, 1, 2)
    o_ml = _SSD_COLS
    o_hg = o_ml + 3 * ML_WIDTH
    o_gate = o_hg + 4 * HG_WIDTH
    rb = SEGMENT_ROWS
    out = {name: _segment_cast(w_t, lambda j, o=o: o + j * rb, width // rb, rb)
           for name, o, width in (('ml', o_ml, 3 * ML_WIDTH), ('hg', o_hg, 4 * HG_WIDTH),
                                  ('gate', o_gate, 3 * w_in.shape[1]))}
    dt_block = SSD_CONV_DIM // LANES
    out['ssd'] = _segment_cast(
        w_t, lambda j: jnp.where(j <= dt_block, j * LANES, (j - 1) * LANES + SSD_HEADS),
        (SSD_CONV_DIM + LANES + SSD_WIDTH) // LANES, LANES, dt_block=dt_block)
    return out


def _layer(x, hn, layer, lb, p, big, next_norm_w, last_layer):
    bsz, seq, d = x.shape
    tb = min(TOKEN_BLOCK, seq)
    assert seq % tb == 0 and tb % CHUNK == 0
    w_ssd, w_ml, w_hg, w_gate = [(big[k], _layer_spec(big[k].shape[1:], layer))
                                 for k in ('ssd', 'ml', 'hg', 'gate')]

    pad_h = (0, LANES - SSD_HEADS)
    ssd_consts = [
        w_ssd, p['ssd_conv_w'].astype(F32), _row(p['ssd_conv_b']),
        _row(jnp.pad(p['ssd_dt_bias'], pad_h)), _row(jnp.pad(p['ssd_a_log'], pad_h)),
        _row(jnp.repeat(p['ssd_d'], SSD_HEAD_DIM)), _row(p['ssd_norm_w'])]
    y_ssd = _mixer_call(
        _ssd_kernel, 'ssd_mixer', hn, ssd_consts, SSD_WIDTH,
        [pltpu.VMEM((SSD_CONV_DIM // LANES, tb + SUBLANES, LANES), F32),
         pltpu.VMEM((tb, SSD_CONV_DIM), F32),
         pltpu.VMEM((tb, LANES), F32), pltpu.VMEM((tb, SSD_WIDTH), F32),
         pltpu.VMEM((SSD_GROUPS, SSD_STATE, SSD_WIDTH // SSD_GROUPS), F32)], tb)

    wqk = jnp.concatenate([p['ml_wq'], p['ml_wk']], axis=-1).astype(BF16)
    wv = p['ml_wv'].astype(BF16)
    wif = jnp.pad(p['ml_w_if'], ((0, 0), (0, LANES - 2 * ML_HEADS))).astype(BF16)
    bif = _row(jnp.pad(p['ml_b_if'], (0, LANES - 2 * ML_HEADS)))
    ml_consts = [w_ml, p['ml_conv_w'].astype(F32), _row(p['ml_conv_b']), wqk, wv, wif, bif,
                 _row(p['ml_norm_w']), _row(p['ml_skip'])]
    y_ml = _mixer_call(
        _ml_kernel, 'mlstm_mixer', hn, ml_consts, ML_WIDTH,
        [pltpu.VMEM((ML_WIDTH // LANES, tb + SUBLANES, LANES), F32), pltpu.VMEM((tb, ML_WIDTH), F32),
         pltpu.VMEM((tb, 2 * ML_WIDTH), F32), pltpu.VMEM((tb, 3 * ML_WIDTH), F32),
         pltpu.VMEM((tb, LANES), F32),
         pltpu.VMEM((ML_HEADS, ML_HEAD_DIM, ML_HEAD_DIM), F32),
         pltpu.VMEM((ML_HEADS, ML_HEAD_DIM, ML_HEAD_DIM), F32), pltpu.VMEM((SUBLANES, LANES), F32)], tb)

    hg_consts = [w_hg, _row(lb), _row(p['hg_norm_w']), jnp.asarray(_hg_level_table())]
    y_hg = _mixer_call(
        _hg_kernel, 'hgrn2_mixer', hn, hg_consts, HG_WIDTH,
        [pltpu.VMEM((tb, 4 * HG_WIDTH), F32), pltpu.VMEM((HG_WIDTH // LANES, CHUNK, LANES), F32),
         pltpu.VMEM((HG_HEADS, HG_HEAD_DIM, HG_HEAD_DIM), F32)], tb)

    merge_consts_a, merge_specs_a = _with_specs([w_gate])
    merge_consts_b, merge_specs_b = _with_specs(
        [(big[k], _layer_spec(big[k].shape[1:], layer))
         for k in ('w_branch_ssd', 'w_branch_ml', 'w_branch_hg', 'w_out')] + [_row(next_norm_w)])
    out_specs = [_tok_spec(tb, d)]
    out_shape = [jax.ShapeDtypeStruct((bsz, seq, d), F32)]
    if not last_layer:
        out_specs.append(_tok_spec(tb, d))
        out_shape.append(jax.ShapeDtypeStruct((bsz, seq, d), BF16))
    outs = pl.pallas_call(
        functools.partial(_merge_kernel, last_layer=last_layer),
        grid=(bsz, seq // tb),
        in_specs=([_tok_spec(tb, d), _tok_spec(tb, d)] + merge_specs_a
                  + [_tok_spec(tb, SSD_WIDTH), _tok_spec(tb, ML_WIDTH), _tok_spec(tb, HG_WIDTH)]
                  + merge_specs_b),
        out_specs=out_specs,
        out_shape=out_shape,
        compiler_params=pltpu.CompilerParams(
            dimension_semantics=("parallel", "parallel"),
            vmem_limit_bytes=VMEM_LIMIT_BYTES),
        name='merge_out',
    )(x, hn, *merge_consts_a, y_ssd, y_ml, y_hg, *merge_consts_b)
    return (outs[0], None) if last_layer else (outs[0], outs[1])


def _prenorm(x, norm_w, tb):
    bsz, seq, d = x.shape
    return pl.pallas_call(
        _prenorm_kernel,
        grid=(bsz, seq // tb),
        in_specs=[_tok_spec(tb, d), _const_spec((1, d))],
        out_specs=_tok_spec(tb, d),
        out_shape=jax.ShapeDtypeStruct((bsz, seq, d), BF16),
        compiler_params=pltpu.CompilerParams(
            dimension_semantics=("parallel", "parallel"), vmem_limit_bytes=VMEM_LIMIT_BYTES),
        name='prenorm',
    )(x, _row(norm_w))


_LAYER_PARAMS = ('ssd_conv_w', 'ssd_conv_b', 'ssd_dt_bias', 'ssd_a_log', 'ssd_d',
                 'ssd_norm_w', 'ml_conv_w', 'ml_conv_b', 'ml_wq', 'ml_wk', 'ml_wv', 'ml_w_if',
                 'ml_b_if', 'ml_norm_w', 'ml_skip', 'hg_norm_w')


def kernel(x, norm_w, w_in, ssd_conv_w, ssd_conv_b, ssd_dt_bias, ssd_a_log, ssd_d, ssd_norm_w, ml_conv_w, ml_conv_b, ml_wq, ml_wk, ml_wv, ml_w_if, ml_b_if, ml_norm_w, ml_skip, hg_lower_bounds, hg_norm_w, w_branch_ssd, w_branch_ml, w_branch_hg, w_out, final_norm_w):
    stacked = dict(ssd_conv_w=ssd_conv_w, ssd_conv_b=ssd_conv_b,
                   ssd_dt_bias=ssd_dt_bias, ssd_a_log=ssd_a_log, ssd_d=ssd_d, ssd_norm_w=ssd_norm_w,
                   ml_conv_w=ml_conv_w, ml_conv_b=ml_conv_b, ml_wq=ml_wq, ml_wk=ml_wk, ml_wv=ml_wv,
                   ml_w_if=ml_w_if, ml_b_if=ml_b_if, ml_norm_w=ml_norm_w, ml_skip=ml_skip,
                   hg_norm_w=hg_norm_w)
    big = dict(**_split_w_in(w_in), w_branch_ssd=w_branch_ssd.astype(BF16),
               w_branch_ml=w_branch_ml.astype(BF16), w_branch_hg=w_branch_hg.astype(BF16),
               w_out=w_out.astype(BF16))
    depth = norm_w.shape[0]
    lbs = jnp.cumsum(jax.nn.softmax(hg_lower_bounds.astype(F32), axis=0), axis=0)
    lbs = lbs - lbs[0]
    hn = _prenorm(x, norm_w[0], min(TOKEN_BLOCK, x.shape[1]))
    for l in range(depth):
        p = {k: stacked[k][l] for k in _LAYER_PARAMS}
        last = l == depth - 1
        x, hn = _layer(x, hn, l, lbs[l], p, big, final_norm_w if last else norm_w[l + 1], last)
    return x
```

```python
import functools

import numpy as np
import jax
import jax.numpy as jnp
from jax import lax
from jax.experimental import pallas as pl
from jax.experimental.pallas import tpu as pltpu

F32 = jnp.float32
BF16 = jnp.bfloat16

EPS = 1e-6
LOG2E = 1.4426950408889634
CONV_K = 4
LANES = 128
SUBLANES = 8
CHUNK = 128
TOKEN_BLOCK = 512
PROJ_ROWS = 256
VMEM_LIMIT_BYTES = 56 * 1024 * 1024

SSD_HEAD_DIM = 64
SSD_HEADS = 16
SSD_GROUPS = 2
SSD_STATE = 128
SSD_WIDTH = SSD_HEADS * SSD_HEAD_DIM
SSD_CONV_DIM = SSD_WIDTH + 2 * SSD_GROUPS * SSD_STATE
ML_HEADS = 4
ML_HEAD_DIM = 128
ML_WIDTH = ML_HEADS * ML_HEAD_DIM
HG_HEADS = 4
HG_HEAD_DIM = 128
HG_WIDTH = HG_HEADS * HG_HEAD_DIM
HG_LEVELS = 7


def _dot(a, b):
    return jnp.dot(a, b, preferred_element_type=F32)


def _dot_nt(a, b):
    return lax.dot_general(a, b, (((1,), (1,)), ((), ())), preferred_element_type=F32)


def _rmsnorm(x, w):
    return x * lax.rsqrt(jnp.mean(x * x, axis=-1, keepdims=True) + EPS) * w


def _softplus(x):
    return jnp.maximum(x, 0.0) + jnp.log1p(jnp.exp(-jnp.abs(x)))


def _log_sigmoid(x):
    return jnp.minimum(x, 0.0) - jnp.log1p(jnp.exp(-jnp.abs(x)))


def _silu(x):
    return x * jax.nn.sigmoid(x)


def _tril_ones_bf16(n):
    r = lax.broadcasted_iota(jnp.int32, (n, n), 0)
    c = lax.broadcasted_iota(jnp.int32, (n, n), 1)
    return jnp.where(c <= r, 1.0, 0.0).astype(BF16)


def _cumsum_time(x, tri):
    hi = x.astype(BF16)
    r1 = x - hi.astype(F32)
    mid = r1.astype(BF16)
    lo = (r1 - mid.astype(F32)).astype(BF16)
    return _dot(tri, hi) + _dot(tri, mid) + _dot(tri, lo)


def _colb(x, j, n=LANES):
    return jnp.broadcast_to(x[:, j:j + 1], (x.shape[0], n))


def _rowb(x, j, m):
    return jnp.broadcast_to(x[j:j + 1, :], (m, x.shape[1]))


def _expand_heads(v):
    rows = v.shape[0]
    lane = lax.broadcasted_iota(jnp.int32, (rows, LANES), 1)
    parts = []
    for j in range(SSD_HEADS // 2):
        a = _colb(v, 2 * j)
        b = _colb(v, 2 * j + 1)
        parts.append(jnp.where(lane < SSD_HEAD_DIM, a, b))
    return jnp.concatenate(parts, axis=1)


def _chunk_start(c):
    return c * CHUNK if isinstance(c, int) else pl.multiple_of(c * CHUNK, CHUNK)


def _for_chunks(n, body):
    for c in range(n):
        body(c, 0)


def _store_col_blocks(dst_ref, first_block, row0, val):
    for i in range(val.shape[1] // LANES):
        dst_ref[first_block + i, pl.ds(row0, val.shape[0]), :] = val[:, i * LANES:(i + 1) * LANES]


def _causal_conv_silu(pre_ref, cw_ref, cb_ref, out_ref, r0, n):
    for blk in range(pre_ref.shape[0]):
        cols = slice(blk * LANES, (blk + 1) * LANES)
        acc = cb_ref[:, cols] + cw_ref[0:1, cols] * pre_ref[blk, pl.ds(r0 + SUBLANES - 3, n), :]
        for j in range(1, CONV_K):
            acc = acc + cw_ref[j:j + 1, cols] * pre_ref[blk, pl.ds(r0 + SUBLANES - 3 + j, n), :]
        out_ref[pl.ds(r0, n), cols] = _silu(acc)


def _conv_carry(pre_ref, tb):
    for blk in range(pre_ref.shape[0]):
        pre_ref[blk, 0:SUBLANES, :] = pre_ref[blk, pl.ds(tb, SUBLANES), :]


def _ssd_kernel(hn_ref, w_ref, wdt_ref, wz_ref, cw_ref, cb_ref, dtb_ref, alog_ref, dskip_ref, gnw_ref,
                y_ref, pre_ref, xc_ref, dt_ref, z_ref, st_ref):
    tb = hn_ref.shape[0]
    L = CHUNK

    @pl.when(pl.program_id(1) == 0)
    def _():
        pre_ref[:, 0:SUBLANES, :] = jnp.zeros((pre_ref.shape[0], SUBLANES, LANES), F32)
        st_ref[...] = jnp.zeros(st_ref.shape, F32)

    group = 4 * LANES
    for r0 in range(0, tb, PROJ_ROWS):
        prow = pl.ds(r0, PROJ_ROWS)
        hn = hn_ref[prow, :]
        for c0 in range(0, SSD_CONV_DIM, group):
            _store_col_blocks(pre_ref, c0 // LANES, SUBLANES + r0, _dot(hn, w_ref[:, c0:c0 + group]))
        dt_ref[prow, :] = _dot(hn, wdt_ref[...])
        z_ref[prow, :] = _dot(hn, wz_ref[...])
        _causal_conv_silu(pre_ref, cw_ref, cb_ref, xc_ref, r0, PROJ_ROWS)
    _conv_carry(pre_ref, tb)

    lane_row = lax.broadcasted_iota(jnp.int32, (1, LANES), 1)
    a_row = jnp.where(lane_row < SSD_HEADS, -jnp.exp(alog_ref[...]) * LOG2E, 0.0)
    r_i = lax.broadcasted_iota(jnp.int32, (L, L), 0)
    c_i = lax.broadcasted_iota(jnp.int32, (L, L), 1)
    causal = c_i <= r_i
    lane = lax.broadcasted_iota(jnp.int32, (L, LANES), 1)
    lo_half = lane < SSD_HEAD_DIM
    tri = _tril_ones_bf16(L)
    gs = SSD_WIDTH // SSD_GROUPS
    pairs_per_group = SSD_HEADS // SSD_GROUPS // 2

    def chunk(c, carry):
        r0 = _chunk_start(c)
        rows = pl.ds(r0, L)
        xs = xc_ref[rows, 0:SSD_WIDTH]
        dt = _softplus(dt_ref[rows, :] + dtb_ref[...])
        cum = _cumsum_time(dt * a_row, tri)
        cum_last = cum[L - 1:L, :]
        r_t = (cum - jnp.log(dt) * LOG2E).T
        dtw_t = (dt * jnp.exp2(cum_last - cum)).T
        dec_last_x = _expand_heads(jnp.exp2(cum_last))
        xs_b = xs.astype(BF16)

        y_parts = []
        for g in range(SSD_GROUPS):
            bm = xc_ref[rows, SSD_WIDTH + g * SSD_STATE:SSD_WIDTH + (g + 1) * SSD_STATE]
            cm = xc_ref[rows, SSD_WIDTH + (SSD_GROUPS + g) * SSD_STATE:
                        SSD_WIDTH + (SSD_GROUPS + g + 1) * SSD_STATE]
            cm_b = cm.astype(BF16)
            cb = _dot_nt(cm_b, bm.astype(BF16))
            bm_t = bm.T
            h_t = st_ref[g]
            y_off = _dot(cm_b, h_t.astype(BF16))
            new_state = []
            for jp in range(pairs_per_group):
                pair = g * pairs_per_group + jp
                lanes = slice(pair * LANES, (pair + 1) * LANES)
                sc, bsc, dec = [], [], []
                for h in (2 * pair, 2 * pair + 1):
                    c_col = _colb(cum, h)
                    seg = c_col - _rowb(r_t, h, L)
                    sc.append((cb * jnp.exp2(jnp.where(causal, seg, -jnp.inf))).astype(BF16))
                    bsc.append((bm_t * _rowb(dtw_t, h, L)).astype(BF16))
                    dec.append(jnp.exp2(c_col))
                xp = xs_b[:, lanes]
                zero = jnp.zeros_like(xp)
                rhs = jnp.concatenate([jnp.where(lo_half, xp, zero), jnp.where(lo_half, zero, xp)], axis=0)
                y_diag = _dot(jnp.concatenate(sc, axis=1), rhs)
                local = _dot(jnp.concatenate(bsc, axis=1), rhs)
                y_parts.append(y_diag + y_off[:, jp * LANES:(jp + 1) * LANES]
                               * jnp.where(lo_half, dec[0], dec[1]))
                new_state.append(h_t[:, jp * LANES:(jp + 1) * LANES] * dec_last_x[:, lanes] + local)
            st_ref[g] = jnp.concatenate(new_state, axis=1)
        y = jnp.concatenate(y_parts, axis=1) + xs * dskip_ref[...]
        y = y * _silu(z_ref[rows, :])
        outs = []
        for g in range(SSD_GROUPS):
            yg = y[:, g * gs:(g + 1) * gs]
            outs.append(yg * lax.rsqrt(jnp.mean(yg * yg, axis=-1, keepdims=True) + EPS))
        y_ref[rows, :] = (jnp.concatenate(outs, axis=1) * gnw_ref[...]).astype(BF16)
        return carry

    _for_chunks(tb // L, chunk)


def _ml_kernel(hn_ref, w_ref, cw_ref, cb_ref, wqk_ref, wv_ref, wif_ref, bif_ref, lnw_ref,
               skip_ref, y_ref, pre_ref, mc_ref, oz_ref, qkv_ref, if_ref, ct_ref, nm_ref, m_ref):
    tb = hn_ref.shape[0]
    L = CHUNK
    D = ML_HEAD_DIM

    @pl.when(pl.program_id(1) == 0)
    def _():
        pre_ref[:, 0:SUBLANES, :] = jnp.zeros((pre_ref.shape[0], SUBLANES, LANES), F32)
        ct_ref[...] = jnp.zeros(ct_ref.shape, F32)
        nm_ref[...] = jnp.zeros(nm_ref.shape, F32)
        m_ref[...] = jnp.zeros(m_ref.shape, F32)

    for r0 in range(0, tb, PROJ_ROWS):
        prow = pl.ds(r0, PROJ_ROWS)
        hn = hn_ref[prow, :]
        _store_col_blocks(pre_ref, 0, SUBLANES + r0, _dot(hn, w_ref[:, 0:ML_WIDTH]))
        oz_ref[prow, :] = _dot(hn, w_ref[:, ML_WIDTH:])
        for h in range(ML_HEADS):
            v = _dot(pre_ref[h, pl.ds(SUBLANES + r0, PROJ_ROWS), :].astype(BF16), wv_ref[h])
            qkv_ref[prow, h * 3 * D + 2 * D:(h + 1) * 3 * D] = v
        _causal_conv_silu(pre_ref, cw_ref, cb_ref, mc_ref, r0, PROJ_ROWS)
        for h in range(ML_HEADS):
            qkv_ref[prow, h * 3 * D:h * 3 * D + 2 * D] = _dot(
                mc_ref[prow, h * D:(h + 1) * D].astype(BF16), wqk_ref[h])
        if_ref[prow, :] = _dot(qkv_ref[prow, :].astype(BF16), wif_ref[...]) + bif_ref[...]
    _conv_carry(pre_ref, tb)

    r_i = lax.broadcasted_iota(jnp.int32, (L, L), 0)
    c_i = lax.broadcasted_iota(jnp.int32, (L, L), 1)
    causal = c_i <= r_i
    tri = _tril_ones_bf16(L)
    scale = D ** -0.5

    ones_b = jnp.ones((L, D), BF16)
    heads = range(ML_HEADS)

    def chunk(c, carry):
        r0 = _chunk_start(c)
        rows = pl.ds(r0, L)
        if_pre = if_ref[rows, :]
        cum = _cumsum_time(_log_sigmoid(if_pre) * LOG2E, tri)
        cum_t = cum.T
        if_t = (if_pre * LOG2E).T
        q_b = [(qkv_ref[rows, h * 3 * D:h * 3 * D + D] * scale).astype(BF16) for h in heads]
        k = [qkv_ref[rows, h * 3 * D + D:h * 3 * D + 2 * D] for h in heads]
        v_aug = [jnp.concatenate([qkv_ref[rows, h * 3 * D + 2 * D:(h + 1) * 3 * D].astype(BF16), ones_b],
                                 axis=1) for h in heads]
        qk = [_dot_nt(q_b[h], k[h].astype(BF16)) for h in heads]
        k_t = [k[h].T for h in heads]
        m_in = [m_ref[h:h + 1, :] for h in heads]
        cn_in = [jnp.concatenate([ct_ref[h], nm_ref[h]], axis=1) for h in heads]
        c_col = [_colb(cum, ML_HEADS + h) for h in heads]
        a_row = [if_t[h:h + 1, :] - cum_t[ML_HEADS + h:ML_HEADS + h + 1, :] for h in heads]
        log_d = [jnp.where(causal, c_col[h] + a_row[h], -jnp.inf) for h in heads]
        log_inter = [c_col[h] + m_in[h] for h in heads]
        m_t = [jnp.maximum(log_inter[h], jnp.max(log_d[h], axis=1, keepdims=True)) for h in heads]
        s_b = [(qk[h] * jnp.exp2(log_d[h] - m_t[h])).astype(BF16) for h in heads]
        w_inter = [jnp.exp2(log_inter[h] - m_t[h]) for h in heads]
        intra = [_dot(s_b[h], v_aug[h]) for h in heads]
        inter = [_dot(q_b[h], cn_in[h].astype(BF16)) for h in heads]
        g_row = [c_col[h][L - 1:L, :] for h in heads]
        log_end = [g_row[h] + a_row[h] for h in heads]
        m_loc = [jnp.max(log_end[h], axis=1, keepdims=True) for h in heads]
        kw_b = [(k_t[h] * jnp.exp2(log_end[h] - m_loc[h])).astype(BF16) for h in heads]
        local = [_dot(kw_b[h], v_aug[h]) for h in heads]
        for h in heads:
            m_new = jnp.maximum(g_row[h] + m_in[h], m_loc[h])
            a_old = jnp.exp2(g_row[h] + m_in[h] - m_new)[:, 0:1]
            a_loc = jnp.exp2(m_loc[h] - m_new)[:, 0:1]
            cn_new = a_old * cn_in[h] + a_loc * local[h]
            ct_ref[h] = cn_new[:, 0:D]
            nm_ref[h] = cn_new[:, D:2 * D]
            m_ref[h:h + 1, :] = m_new
        hh = []
        for h in heads:
            both = intra[h] + jnp.concatenate([w_inter[h], w_inter[h]], axis=1) * inter[h]
            den = jnp.maximum(jnp.abs(both[:, D:2 * D]), jnp.exp2(-m_t[h]))
            hh.append(both[:, 0:D] / den)
        mu = [jnp.mean(hh[h], axis=-1, keepdims=True) for h in heads]
        xc = [hh[h] - mu[h] for h in heads]
        var = [jnp.mean(xc[h] * xc[h], axis=-1, keepdims=True) for h in heads]
        for h in heads:
            cols = slice(h * D, (h + 1) * D)
            ln = xc[h] * lax.rsqrt(var[h] + EPS) * lnw_ref[:, cols]
            o_gate = jax.nn.sigmoid(oz_ref[rows, h * D:(h + 1) * D])
            z = oz_ref[rows, ML_WIDTH + h * D:ML_WIDTH + (h + 1) * D]
            out = (ln * o_gate + skip_ref[:, cols] * mc_ref[rows, cols]) * _silu(z)
            y_ref[rows, cols] = out.astype(BF16)
        return carry

    _for_chunks(tb // L, chunk)


def _hg_level_table():
    l = np.arange(CHUNK)[:, None]
    s = np.arange(CHUNK)[None, :]
    x = l ^ s
    msb = np.floor(np.log2(np.maximum(x, 1))).astype(np.int32)
    return np.where(s < l, msb, np.where(s == l, HG_LEVELS, HG_LEVELS + 1)).astype(np.int32)


def _replicated_row(ref, blk, r):
    return ref[blk, pl.ds(r, SUBLANES, stride=0), :]


def _hg_reference_rows(cum_ref, level):
    b = 1 << level
    sub = lax.broadcasted_iota(jnp.int32, (SUBLANES, LANES), 0)
    col_blocks = []
    for blk in range(cum_ref.shape[0]):
        def row8(r, blk=blk):
            return _replicated_row(cum_ref, blk, r)
        pieces = []
        if 2 * b >= SUBLANES:
            for i in range(CHUNK // (2 * b)):
                pieces.extend([row8(i * 2 * b + b - 1)] * (2 * b // SUBLANES))
        else:
            for grp in range(CHUNK // SUBLANES):
                base = grp * SUBLANES
                acc = row8(base + b - 1)
                for i in range(1, SUBLANES // (2 * b)):
                    acc = jnp.where(sub >= i * 2 * b, row8(base + i * 2 * b + b - 1), acc)
                pieces.append(acc)
        col_blocks.append(jnp.concatenate(pieces, axis=0))
    return jnp.concatenate(col_blocks, axis=1)


def _hg_kernel(hn_ref, w_ref, lb_ref, gnw_ref, lv_ref, y_ref, pj_ref, cum_ref, st_ref):
    tb = hn_ref.shape[0]
    L = CHUNK
    D = HG_HEAD_DIM
    W = HG_WIDTH

    @pl.when(pl.program_id(1) == 0)
    def _():
        st_ref[...] = jnp.zeros(st_ref.shape, F32)

    pj_ref[...] = _dot(hn_ref[...], w_ref[...])
    tri = _tril_ones_bf16(L)
    lb = lb_ref[...]
    lb_pos = lb > 0.0

    def chunk(c, carry):
        r0 = _chunk_start(c)
        rows = pl.ds(r0, L)
        fx = pj_ref[rows, W:2 * W]
        a = jnp.abs(fx)
        t = jnp.exp(-a)
        pos = fx >= 0.0
        one_t = 1.0 + t
        log_num = jnp.where(pos, jnp.log(1.0 + lb * t), jnp.where(lb_pos, jnp.log(t + lb), -a))
        log2_f = (log_num - jnp.log(one_t)) * LOG2E
        k = (1.0 - lb) * jnp.where(pos, t, 1.0) / one_t
        q = _silu(pj_ref[rows, 0:W])
        v_b = pj_ref[rows, 2 * W:3 * W].astype(BF16)
        cum = _cumsum_time(log2_f, tri)
        _store_col_blocks(cum_ref, 0, 0, cum)
        lv = lv_ref[...]
        q_b = q.astype(BF16)
        k_b = k.astype(BF16)
        attn = []
        for h in range(HG_HEADS):
            cols = slice(h * D, (h + 1) * D)
            attn.append(jnp.where(lv == HG_LEVELS, _dot_nt(q_b[:, cols], k_b[:, cols]), 0.0))
        for level in range(HG_LEVELS):
            e = jnp.exp2(-jnp.abs(cum - _hg_reference_rows(cum_ref, level)))
            qe = (q * e).astype(BF16)
            ke = (k * e).astype(BF16)
            for h in range(HG_HEADS):
                cols = slice(h * D, (h + 1) * D)
                attn[h] = jnp.where(lv == level, _dot_nt(qe[:, cols], ke[:, cols]), attn[h])
        cum_last = cum[L - 1:L, :]
        q_dec = (q * jnp.exp2(cum)).astype(BF16)
        k_end = k * jnp.exp2(cum_last - cum)
        dec_last = jnp.exp2(cum_last)
        gz = _silu(pj_ref[rows,3 * W:4 * W])
        for h in range(HG_HEADS):
            cols = slice(h * D, (h + 1) * D)
            st = st_ref[h]
            o = _dot(attn[h].astype(BF16), v_b[:, cols]) + _dot_nt(q_dec[:, cols], st.astype(BF16))
            v_t = pj_ref[rows,2 * W + h * D:2 * W + (h + 1) * D].T.astype(BF16)
            st_ref[h] = st * dec_last[:, cols] + _dot(v_t, k_end[:, cols].astype(BF16))
            on = o * lax.rsqrt(jnp.mean(o * o, axis=-1, keepdims=True) + EPS)
            y_ref[rows, cols] = (on * gnw_ref[:, cols] * gz[:, cols]).astype(BF16)
        return carry

    _for_chunks(tb // L, chunk)


def _merge_kernel(x_ref, hn_ref, wg_ref, ys_ref, ym_ref, yh_ref, wbs_ref, wbm_ref, wbh_ref, wo_ref,
                  nnw_ref, o_ref, *hn_next_ref, last_layer):
    x = x_ref[...]
    hn = hn_ref[...]
    d = x.shape[-1]
    merged = jax.nn.sigmoid(_dot(hn, wg_ref[:, 0:d])) * _dot(ys_ref[...], wbs_ref[...])
    merged = merged + jax.nn.sigmoid(_dot(hn, wg_ref[:, d:2 * d])) * _dot(ym_ref[...], wbm_ref[...])
    merged = merged + jax.nn.sigmoid(_dot(hn, wg_ref[:, 2 * d:3 * d])) * _dot(yh_ref[...], wbh_ref[...])
    out = x + _dot(merged.astype(BF16), wo_ref[...])
    if last_layer:
        o_ref[...] = _rmsnorm(out, nnw_ref[...])
    else:
        o_ref[...] = out
        hn_next_ref[0][...] = _rmsnorm(out, nnw_ref[...]).astype(BF16)


def _prenorm_kernel(x_ref, nw_ref, hn_ref):
    hn_ref[...] = _rmsnorm(x_ref[...], nw_ref[...]).astype(BF16)


def _segment_cast_kernel(w_ref, o_ref, *, keep_rows):
    w = w_ref[0]
    if keep_rows is not None:
        row = lax.broadcasted_iota(jnp.int32, w.shape, 0)
        w = jnp.where(row < keep_rows, w, 0.0)
    o_ref[...] = w.T.astype(BF16)


def _const_spec(shape):
    nd = len(shape)
    return pl.BlockSpec(shape, lambda b, t: (0,) * nd)


def _tok_spec(tb, width):
    return pl.BlockSpec((None, tb, width), lambda b, t: (b, t, 0))


def _layer_spec(block_shape, layer, col_block=0):
    nd = len(block_shape)
    return pl.BlockSpec((None,) + tuple(block_shape),
                        lambda b, t: (layer,) + (0,) * (nd - 1) + (col_block,))


def _with_specs(consts):
    pairs = [c if isinstance(c, tuple) else (c, _const_spec(c.shape)) for c in consts]
    return [a for a, _ in pairs], [sp for _, sp in pairs]


def _mixer_call(kernel_fn, name, x, consts, out_width, scratch_shapes, tb):
    bsz, seq, d = x.shape
    consts, const_specs = _with_specs(consts)
    return pl.pallas_call(
        kernel_fn,
        grid=(bsz, seq // tb),
        in_specs=[_tok_spec(tb, d)] + const_specs,
        out_specs=_tok_spec(tb, out_width),
        out_shape=jax.ShapeDtypeStruct((bsz, seq, out_width), BF16),
        scratch_shapes=scratch_shapes,
        compiler_params=pltpu.CompilerParams(
            dimension_semantics=("parallel", "arbitrary"),
            vmem_limit_bytes=VMEM_LIMIT_BYTES),
        name=name,
    )(x, *consts)


def _row(v):
    return v.reshape(1, -1).astype(F32)


SEGMENT_ROWS = 512


def _segment_cast(w_t, src, width, keep_rows=None):
    depth, _, d = w_t.shape
    rb = min(SEGMENT_ROWS, width)
    return pl.pallas_call(
        functools.partial(_segment_cast_kernel, keep_rows=keep_rows),
        grid=(depth, width // rb),
        in_specs=[pl.BlockSpec((pl.Element(1), pl.Element(rb), pl.Element(d)),
                               lambda l, j: (l, pl.multiple_of(src + j * rb, 16), 0))],
        out_specs=pl.BlockSpec((None, d, rb), lambda l, j: (l, 0, j)),
        out_shape=jax.ShapeDtypeStruct((depth, d, width), BF16),
        compiler_params=pltpu.CompilerParams(
            dimension_semantics=("parallel", "parallel"), vmem_limit_bytes=VMEM_LIMIT_BYTES),
        name='w_in_segment_cast',
    )(w_t)


def _split_w_in(w_in):
    w_t = jnp.swapaxes(w_in, 1, 2)
    o_dt = SSD_CONV_DIM
    o_z = o_dt + SSD_HEADS
    o_ml = o_z + SSD_WIDTH
    o_hg = o_ml + 3 * ML_WIDTH
    o_gate = o_hg + 4 * HG_WIDTH
    return dict(ssd=_segment_cast(w_t, 0, SSD_CONV_DIM),
                ssd_dt=_segment_cast(w_t, o_dt, LANES, keep_rows=SSD_HEADS),
                ssd_z=_segment_cast(w_t, o_z, SSD_WIDTH),
                ml=_segment_cast(w_t, o_ml, 3 * ML_WIDTH),
                hg=_segment_cast(w_t, o_hg, 4 * HG_WIDTH),
                gate=_segment_cast(w_t, o_gate, 3 * w_in.shape[1]))


def _layer(x, hn, layer, lb, p, big, next_norm_w, last_layer):
    bsz, seq, d = x.shape
    tb = min(TOKEN_BLOCK, seq)
    assert seq % tb == 0 and tb % CHUNK == 0
    w_ssd, w_ssd_dt, w_ssd_z, w_ml, w_hg, w_gate = [
        (big[k], _layer_spec(big[k].shape[1:], layer)) for k in ('ssd', 'ssd_dt', 'ssd_z', 'ml', 'hg', 'gate')]

    pad_h = (0, LANES - SSD_HEADS)
    ssd_consts = [
        w_ssd, w_ssd_dt, w_ssd_z, p['ssd_conv_w'].astype(F32), _row(p['ssd_conv_b']),
        _row(jnp.pad(p['ssd_dt_bias'], pad_h)), _row(jnp.pad(p['ssd_a_log'], pad_h)),
        _row(jnp.repeat(p['ssd_d'], SSD_HEAD_DIM)), _row(p['ssd_norm_w'])]
    y_ssd = _mixer_call(
        _ssd_kernel, 'ssd_mixer', hn, ssd_consts, SSD_WIDTH,
        [pltpu.VMEM((SSD_CONV_DIM // LANES, tb + SUBLANES, LANES), F32),
         pltpu.VMEM((tb, SSD_CONV_DIM), F32),
         pltpu.VMEM((tb, LANES), F32), pltpu.VMEM((tb, SSD_WIDTH), F32),
         pltpu.VMEM((SSD_GROUPS, SSD_STATE, SSD_WIDTH // SSD_GROUPS), F32)], tb)

    wqk = jnp.concatenate([p['ml_wq'], p['ml_wk']], axis=-1).astype(BF16)
    wv = p['ml_wv'].astype(BF16)
    wif = jnp.pad(p['ml_w_if'], ((0, 0), (0, LANES - 2 * ML_HEADS))).astype(BF16)
    bif = _row(jnp.pad(p['ml_b_if'], (0, LANES - 2 * ML_HEADS)))
    ml_consts = [w_ml, p['ml_conv_w'].astype(F32), _row(p['ml_conv_b']), wqk, wv, wif, bif,
                 _row(p['ml_norm_w']), _row(p['ml_skip'])]
    y_ml = _mixer_call(
        _ml_kernel, 'mlstm_mixer', hn, ml_consts, ML_WIDTH,
        [pltpu.VMEM((ML_WIDTH // LANES, tb + SUBLANES, LANES), F32), pltpu.VMEM((tb, ML_WIDTH), F32),
         pltpu.VMEM((tb, 2 * ML_WIDTH), F32), pltpu.VMEM((tb, 3 * ML_WIDTH), F32),
         pltpu.VMEM((tb, LANES), F32),
         pltpu.VMEM((ML_HEADS, ML_HEAD_DIM, ML_HEAD_DIM), F32),
         pltpu.VMEM((ML_HEADS, ML_HEAD_DIM, ML_HEAD_DIM), F32), pltpu.VMEM((SUBLANES, LANES), F32)], tb)

    hg_consts = [w_hg, _row(lb), _row(p['hg_norm_w']), jnp.asarray(_hg_level_table())]
    y_hg = _mixer_call(
        _hg_kernel, 'hgrn2_mixer', hn, hg_consts, HG_WIDTH,
        [pltpu.VMEM((tb, 4 * HG_WIDTH), F32), pltpu.VMEM((HG_WIDTH // LANES, CHUNK, LANES), F32),
         pltpu.VMEM((HG_HEADS, HG_HEAD_DIM, HG_HEAD_DIM), F32)], tb)

    merge_consts_a, merge_specs_a = _with_specs([w_gate])
    merge_consts_b, merge_specs_b = _with_specs(
        [(big[k], _layer_spec(big[k].shape[1:], layer))
         for k in ('w_branch_ssd', 'w_branch_ml', 'w_branch_hg', 'w_out')] + [_row(next_norm_w)])
    out_specs = [_tok_spec(tb, d)]
    out_shape = [jax.ShapeDtypeStruct((bsz, seq, d), F32)]
    if not last_layer:
        out_specs.append(_tok_spec(tb, d))
        out_shape.append(jax.ShapeDtypeStruct((bsz, seq, d), BF16))
    outs = pl.pallas_call(
        functools.partial(_merge_kernel, last_layer=last_layer),
        grid=(bsz, seq // tb),
        in_specs=([_tok_spec(tb, d), _tok_spec(tb, d)] + merge_specs_a
                  + [_tok_spec(tb, SSD_WIDTH), _tok_spec(tb, ML_WIDTH), _tok_spec(tb, HG_WIDTH)]
                  + merge_specs_b),
        out_specs=out_specs,
        out_shape=out_shape,
        compiler_params=pltpu.CompilerParams(
            dimension_semantics=("parallel", "parallel"),
            vmem_limit_bytes=VMEM_LIMIT_BYTES),
        name='merge_out',
    )(x, hn, *merge_consts_a, y_ssd, y_ml, y_hg, *merge_consts_b)
    return (outs[0], None) if last_layer else (outs[0], outs[1])


def _prenorm(x, norm_w, tb):
    bsz, seq, d = x.shape
    return pl.pallas_call(
        _prenorm_kernel,
        grid=(bsz, seq // tb),
        in_specs=[_tok_spec(tb, d), _const_spec((1, d))],
        out_specs=_tok_spec(tb, d),
        out_shape=jax.ShapeDtypeStruct((bsz, seq, d), BF16),
        compiler_params=pltpu.CompilerParams(
            dimension_semantics=("parallel", "parallel"), vmem_limit_bytes=VMEM_LIMIT_BYTES),
        name='prenorm',
    )(x, _row(norm_w))


_LAYER_PARAMS = ('ssd_conv_w', 'ssd_conv_b', 'ssd_dt_bias', 'ssd_a_log', 'ssd_d',
                 'ssd_norm_w', 'ml_conv_w', 'ml_conv_b', 'ml_wq', 'ml_wk', 'ml_wv', 'ml_w_if',
                 'ml_b_if', 'ml_norm_w', 'ml_skip', 'hg_norm_w')


def kernel(x, norm_w, w_in, ssd_conv_w, ssd_conv_b, ssd_dt_bias, ssd_a_log, ssd_d, ssd_norm_w, ml_conv_w, ml_conv_b, ml_wq, ml_wk, ml_wv, ml_w_if, ml_b_if, ml_norm_w, ml_skip, hg_lower_bounds, hg_norm_w, w_branch_ssd, w_branch_ml, w_branch_hg, w_out, final_norm_w):
    stacked = dict(ssd_conv_w=ssd_conv_w, ssd_conv_b=ssd_conv_b,
                   ssd_dt_bias=ssd_dt_bias, ssd_a_log=ssd_a_log, ssd_d=ssd_d, ssd_norm_w=ssd_norm_w,
                   ml_conv_w=ml_conv_w, ml_conv_b=ml_conv_b, ml_wq=ml_wq, ml_wk=ml_wk, ml_wv=ml_wv,
                   ml_w_if=ml_w_if, ml_b_if=ml_b_if, ml_norm_w=ml_norm_w, ml_skip=ml_skip,
                   hg_norm_w=hg_norm_w)
    big = dict(**_split_w_in(w_in), w_branch_ssd=w_branch_ssd.astype(BF16),
               w_branch_ml=w_branch_ml.astype(BF16), w_branch_hg=w_branch_hg.astype(BF16),
               w_out=w_out.astype(BF16))
    depth = norm_w.shape[0]
    lbs = jnp.cumsum(jax.nn.softmax(hg_lower_bounds.astype(F32), axis=0), axis=0)
    lbs = lbs - lbs[0]
    hn = _prenorm(x, norm_w[0], min(TOKEN_BLOCK, x.shape[1]))
    for l in range(depth):
        p = {k: stacked[k][l] for k in _LAYER_PARAMS}
        last = l == depth - 1
        x, hn = _layer(x, hn, l, lbs[l], p, big, final_norm_w if last else norm_w[l + 1], last)
    return x
```

```python
import functools

import numpy as np
import jax
import jax.numpy as jnp
from jax import lax
from jax.experimental import pallas as pl
from jax.experimental.pallas import tpu as pltpu

F32 = jnp.float32
BF16 = jnp.bfloat16

EPS = 1e-6
LOG2E = 1.4426950408889634
CONV_K = 4
LANES = 128
SUBLANES = 8
CHUNK = 128
TOKEN_BLOCK = 1024
MERGE_BLOCK = 512
PROJ_ROWS = 256
VMEM_LIMIT_BYTES = 56 * 1024 * 1024

SSD_HEAD_DIM = 64
SSD_HEADS = 16
SSD_GROUPS = 2
SSD_STATE = 128
SSD_WIDTH = SSD_HEADS * SSD_HEAD_DIM
SSD_CONV_DIM = SSD_WIDTH + 2 * SSD_GROUPS * SSD_STATE
ML_HEADS = 4
ML_HEAD_DIM = 128
ML_WIDTH = ML_HEADS * ML_HEAD_DIM
HG_HEADS = 4
HG_HEAD_DIM = 128
HG_WIDTH = HG_HEADS * HG_HEAD_DIM
HG_LEVELS = 7


def _dot(a, b):
    return jnp.dot(a, b, preferred_element_type=F32)


def _dot_nt(a, b):
    return lax.dot_general(a, b, (((1,), (1,)), ((), ())), preferred_element_type=F32)


def _rmsnorm(x, w):
    return x * lax.rsqrt(jnp.mean(x * x, axis=-1, keepdims=True) + EPS) * w


def _softplus(x):
    return jnp.maximum(x, 0.0) + jnp.log1p(jnp.exp(-jnp.abs(x)))


def _log_sigmoid(x):
    return jnp.minimum(x, 0.0) - jnp.log1p(jnp.exp(-jnp.abs(x)))


def _silu(x):
    return x * jax.nn.sigmoid(x)


def _tril_ones_bf16(n):
    r = lax.broadcasted_iota(jnp.int32, (n, n), 0)
    c = lax.broadcasted_iota(jnp.int32, (n, n), 1)
    return jnp.where(c <= r, 1.0, 0.0).astype(BF16)


def _cumsum_time(x, tri):
    hi = x.astype(BF16)
    r1 = x - hi.astype(F32)
    mid = r1.astype(BF16)
    lo = (r1 - mid.astype(F32)).astype(BF16)
    return _dot(tri, hi) + _dot(tri, mid) + _dot(tri, lo)


def _colb(x, j, n=LANES):
    return jnp.broadcast_to(x[:, j:j + 1], (x.shape[0], n))


def _rowb(x, j, m):
    return jnp.broadcast_to(x[j:j + 1, :], (m, x.shape[1]))


def _expand_heads(v):
    rows = v.shape[0]
    lane = lax.broadcasted_iota(jnp.int32, (rows, LANES), 1)
    parts = []
    for j in range(SSD_HEADS // 2):
        a = _colb(v, 2 * j)
        b = _colb(v, 2 * j + 1)
        parts.append(jnp.where(lane < SSD_HEAD_DIM, a, b))
    return jnp.concatenate(parts, axis=1)


def _chunk_start(c):
    return c * CHUNK if isinstance(c, int) else pl.multiple_of(c * CHUNK, CHUNK)


def _for_chunks(n, body):
    for c in range(n):
        body(c, 0)


def _store_col_blocks(dst_ref, first_block, row0, val):
    for i in range(val.shape[1] // LANES):
        dst_ref[first_block + i, pl.ds(row0, val.shape[0]), :] = val[:, i * LANES:(i + 1) * LANES]


def _causal_conv_silu(pre_ref, cw_ref, cb_ref, out_ref, r0, n):
    for blk in range(pre_ref.shape[0]):
        cols = slice(blk * LANES, (blk + 1) * LANES)
        acc = cb_ref[:, cols] + cw_ref[0:1, cols] * pre_ref[blk, pl.ds(r0 + SUBLANES - 3, n), :]
        for j in range(1, CONV_K):
            acc = acc + cw_ref[j:j + 1, cols] * pre_ref[blk, pl.ds(r0 + SUBLANES - 3 + j, n), :]
        out_ref[pl.ds(r0, n), cols] = _silu(acc)


def _conv_carry(pre_ref, tb):
    for blk in range(pre_ref.shape[0]):
        pre_ref[blk, 0:SUBLANES, :] = pre_ref[blk, pl.ds(tb, SUBLANES), :]


def _ssd_kernel(hn_ref, w_ref, wdt_ref, wz_ref, cw_ref, cb_ref, dtb_ref, alog_ref, dskip_ref, gnw_ref,
                y_ref, pre_ref, xc_ref, dt_ref, z_ref, st_ref):
    tb = hn_ref.shape[0]
    L = CHUNK

    @pl.when(pl.program_id(1) == 0)
    def _():
        pre_ref[:, 0:SUBLANES, :] = jnp.zeros((pre_ref.shape[0], SUBLANES, LANES), F32)
        st_ref[...] = jnp.zeros(st_ref.shape, F32)

    group = 4 * LANES
    for r0 in range(0, tb, PROJ_ROWS):
        prow = pl.ds(r0, PROJ_ROWS)
        hn = hn_ref[prow, :]
        for c0 in range(0, SSD_CONV_DIM, group):
            _store_col_blocks(pre_ref, c0 // LANES, SUBLANES + r0, _dot(hn, w_ref[:, c0:c0 + group]))
        dt_ref[prow, :] = _dot(hn, wdt_ref[...])
        z_ref[prow, :] = _dot(hn, wz_ref[...])
        _causal_conv_silu(pre_ref, cw_ref, cb_ref, xc_ref, r0, PROJ_ROWS)
    _conv_carry(pre_ref, tb)

    lane_row = lax.broadcasted_iota(jnp.int32, (1, LANES), 1)
    a_row = jnp.where(lane_row < SSD_HEADS, -jnp.exp(alog_ref[...]) * LOG2E, 0.0)
    r_i = lax.broadcasted_iota(jnp.int32, (L, L), 0)
    c_i = lax.broadcasted_iota(jnp.int32, (L, L), 1)
    causal = c_i <= r_i
    lane = lax.broadcasted_iota(jnp.int32, (L, LANES), 1)
    lo_half = lane < SSD_HEAD_DIM
    tri = _tril_ones_bf16(L)
    gs = SSD_WIDTH // SSD_GROUPS
    pairs_per_group = SSD_HEADS // SSD_GROUPS // 2

    def chunk(c, carry):
        r0 = _chunk_start(c)
        rows = pl.ds(r0, L)
        xs = xc_ref[rows, 0:SSD_WIDTH]
        dt = _softplus(dt_ref[rows, :] + dtb_ref[...])
        cum = _cumsum_time(dt * a_row, tri)
        cum_last = cum[L - 1:L, :]
        r_t = (cum - jnp.log(dt) * LOG2E).T
        dtw_t = (dt * jnp.exp2(cum_last - cum)).T
        dec_last_x = _expand_heads(jnp.exp2(cum_last))
        xs_b = xs.astype(BF16)

        y_parts = []
        for g in range(SSD_GROUPS):
            bm = xc_ref[rows, SSD_WIDTH + g * SSD_STATE:SSD_WIDTH + (g + 1) * SSD_STATE]
            cm = xc_ref[rows, SSD_WIDTH + (SSD_GROUPS + g) * SSD_STATE:
                        SSD_WIDTH + (SSD_GROUPS + g + 1) * SSD_STATE]
            cm_b = cm.astype(BF16)
            cb = _dot_nt(cm_b, bm.astype(BF16))
            bm_t = bm.T
            h_t = st_ref[g]
            y_off = _dot(cm_b, h_t.astype(BF16))
            new_state = []
            for jp in range(pairs_per_group):
                pair = g * pairs_per_group + jp
                lanes = slice(pair * LANES, (pair + 1) * LANES)
                sc, bsc, dec = [], [], []
                for h in (2 * pair, 2 * pair + 1):
                    c_col = _colb(cum, h)
                    seg = c_col - _rowb(r_t, h, L)
                    sc.append((cb * jnp.exp2(jnp.where(causal, seg, -jnp.inf))).astype(BF16))
                    bsc.append((bm_t * _rowb(dtw_t, h, L)).astype(BF16))
                    dec.append(jnp.exp2(c_col))
                xp = xs_b[:, lanes]
                zero = jnp.zeros_like(xp)
                rhs = jnp.concatenate([jnp.where(lo_half, xp, zero), jnp.where(lo_half, zero, xp)], axis=0)
                y_diag = _dot(jnp.concatenate(sc, axis=1), rhs)
                local = _dot(jnp.concatenate(bsc, axis=1), rhs)
                y_parts.append(y_diag + y_off[:, jp * LANES:(jp + 1) * LANES]
                               * jnp.where(lo_half, dec[0], dec[1]))
                new_state.append(h_t[:, jp * LANES:(jp + 1) * LANES] * dec_last_x[:, lanes] + local)
            st_ref[g] = jnp.concatenate(new_state, axis=1)
        y = jnp.concatenate(y_parts, axis=1) + xs * dskip_ref[...]
        y = y * _silu(z_ref[rows, :])
        outs = []
        for g in range(SSD_GROUPS):
            yg = y[:, g * gs:(g + 1) * gs]
            outs.append(yg * lax.rsqrt(jnp.mean(yg * yg, axis=-1, keepdims=True) + EPS))
        y_ref[rows, :] = (jnp.concatenate(outs, axis=1) * gnw_ref[...]).astype(BF16)
        return carry

    _for_chunks(tb // L, chunk)


def _ml_kernel(hn_ref, w_ref, cw_ref, cb_ref, wqk_ref, wv_ref, wif_ref, bif_ref, lnw_ref,
               skip_ref, y_ref, pre_ref, mc_ref, oz_ref, qkv_ref, if_ref, ct_ref, nm_ref, m_ref):
    tb = hn_ref.shape[0]
    L = CHUNK
    D = ML_HEAD_DIM

    @pl.when(pl.program_id(1) == 0)
    def _():
        pre_ref[:, 0:SUBLANES, :] = jnp.zeros((pre_ref.shape[0], SUBLANES, LANES), F32)
        ct_ref[...] = jnp.zeros(ct_ref.shape, F32)
        nm_ref[...] = jnp.zeros(nm_ref.shape, F32)
        m_ref[...] = jnp.zeros(m_ref.shape, F32)

    for r0 in range(0, tb, PROJ_ROWS):
        prow = pl.ds(r0, PROJ_ROWS)
        hn = hn_ref[prow, :]
        _store_col_blocks(pre_ref, 0, SUBLANES + r0, _dot(hn, w_ref[:, 0:ML_WIDTH]))
        oz_ref[prow, :] = _dot(hn, w_ref[:, ML_WIDTH:])
        for h in range(ML_HEADS):
            v = _dot(pre_ref[h, pl.ds(SUBLANES + r0, PROJ_ROWS), :].astype(BF16), wv_ref[h])
            qkv_ref[prow, h * 3 * D + 2 * D:(h + 1) * 3 * D] = v
        _causal_conv_silu(pre_ref, cw_ref, cb_ref, mc_ref, r0, PROJ_ROWS)
        for h in range(ML_HEADS):
            qkv_ref[prow, h * 3 * D:h * 3 * D + 2 * D] = _dot(
                mc_ref[prow, h * D:(h + 1) * D].astype(BF16), wqk_ref[h])
        if_ref[prow, :] = _dot(qkv_ref[prow, :].astype(BF16), wif_ref[...]) + bif_ref[...]
    _conv_carry(pre_ref, tb)

    r_i = lax.broadcasted_iota(jnp.int32, (L, L), 0)
    c_i = lax.broadcasted_iota(jnp.int32, (L, L), 1)
    causal = c_i <= r_i
    tri = _tril_ones_bf16(L)
    scale = D ** -0.5

    ones_b = jnp.ones((L, D), BF16)
    heads = range(ML_HEADS)

    def chunk(c, carry):
        r0 = _chunk_start(c)
        rows = pl.ds(r0, L)
        if_pre = if_ref[rows, :]
        cum = _cumsum_time(_log_sigmoid(if_pre) * LOG2E, tri)
        cum_t = cum.T
        if_t = (if_pre * LOG2E).T
        q_b = [(qkv_ref[rows, h * 3 * D:h * 3 * D + D] * scale).astype(BF16) for h in heads]
        k = [qkv_ref[rows, h * 3 * D + D:h * 3 * D + 2 * D] for h in heads]
        v_aug = [jnp.concatenate([qkv_ref[rows, h * 3 * D + 2 * D:(h + 1) * 3 * D].astype(BF16), ones_b],
                                 axis=1) for h in heads]
        qk = [_dot_nt(q_b[h], k[h].astype(BF16)) for h in heads]
        k_t = [k[h].T for h in heads]
        m_in = [m_ref[h:h + 1, :] for h in heads]
        cn_in = [jnp.concatenate([ct_ref[h], nm_ref[h]], axis=1) for h in heads]
        c_col = [_colb(cum, ML_HEADS + h) for h in heads]
        a_row = [if_t[h:h + 1, :] - cum_t[ML_HEADS + h:ML_HEADS + h + 1, :] for h in heads]
        log_d = [jnp.where(causal, c_col[h] + a_row[h], -jnp.inf) for h in heads]
        log_inter = [c_col[h] + m_in[h] for h in heads]
        m_t = [jnp.maximum(log_inter[h], jnp.max(log_d[h], axis=1, keepdims=True)) for h in heads]
        s_b = [(qk[h] * jnp.exp2(log_d[h] - m_t[h])).astype(BF16) for h in heads]
        w_inter = [jnp.exp2(log_inter[h] - m_t[h]) for h in heads]
        intra = [_dot(s_b[h], v_aug[h]) for h in heads]
        inter = [_dot(q_b[h], cn_in[h].astype(BF16)) for h in heads]
        g_row = [c_col[h][L - 1:L, :] for h in heads]
        log_end = [g_row[h] + a_row[h] for h in heads]
        m_loc = [jnp.max(log_end[h], axis=1, keepdims=True) for h in heads]
        kw_b = [(k_t[h] * jnp.exp2(log_end[h] - m_loc[h])).astype(BF16) for h in heads]
        local = [_dot(kw_b[h], v_aug[h]) for h in heads]
        for h in heads:
            m_new = jnp.maximum(g_row[h] + m_in[h], m_loc[h])
            a_old = jnp.exp2(g_row[h] + m_in[h] - m_new)[:, 0:1]
            a_loc = jnp.exp2(m_loc[h] - m_new)[:, 0:1]
            cn_new = a_old * cn_in[h] + a_loc * local[h]
            ct_ref[h] = cn_new[:, 0:D]
            nm_ref[h] = cn_new[:, D:2 * D]
            m_ref[h:h + 1, :] = m_new
        hh = []
        for h in heads:
            both = intra[h] + jnp.concatenate([w_inter[h], w_inter[h]], axis=1) * inter[h]
            den = jnp.maximum(jnp.abs(both[:, D:2 * D]), jnp.exp2(-m_t[h]))
            hh.append(both[:, 0:D] / den)
        mu = [jnp.mean(hh[h], axis=-1, keepdims=True) for h in heads]
        xc = [hh[h] - mu[h] for h in heads]
        var = [jnp.mean(xc[h] * xc[h], axis=-1, keepdims=True) for h in heads]
        for h in heads:
            cols = slice(h * D, (h + 1) * D)
            ln = xc[h] * lax.rsqrt(var[h] + EPS) * lnw_ref[:, cols]
            o_gate = jax.nn.sigmoid(oz_ref[rows, h * D:(h + 1) * D])
            z = oz_ref[rows, ML_WIDTH + h * D:ML_WIDTH + (h + 1) * D]
            out = (ln * o_gate + skip_ref[:, cols] * mc_ref[rows, cols]) * _silu(z)
            y_ref[rows, cols] = out.astype(BF16)
        return carry

    _for_chunks(tb // L, chunk)


def _hg_level_table():
    l = np.arange(CHUNK)[:, None]
    s = np.arange(CHUNK)[None, :]
    x = l ^ s
    msb = np.floor(np.log2(np.maximum(x, 1))).astype(np.int32)
    return np.where(s < l, msb, np.where(s == l, HG_LEVELS, HG_LEVELS + 1)).astype(np.int32)


def _replicated_row(ref, blk, r):
    return ref[blk, pl.ds(r, SUBLANES, stride=0), :]


def _hg_reference_rows(cum_ref, level):
    b = 1 << level
    sub = lax.broadcasted_iota(jnp.int32, (SUBLANES, LANES), 0)
    col_blocks = []
    for blk in range(cum_ref.shape[0]):
        def row8(r, blk=blk):
            return _replicated_row(cum_ref, blk, r)
        pieces = []
        if 2 * b >= SUBLANES:
            for i in range(CHUNK // (2 * b)):
                pieces.extend([row8(i * 2 * b + b - 1)] * (2 * b // SUBLANES))
        else:
            for grp in range(CHUNK // SUBLANES):
                base = grp * SUBLANES
                acc = row8(base + b - 1)
                for i in range(1, SUBLANES // (2 * b)):
                    acc = jnp.where(sub >= i * 2 * b, row8(base + i * 2 * b + b - 1), acc)
                pieces.append(acc)
        col_blocks.append(jnp.concatenate(pieces, axis=0))
    return jnp.concatenate(col_blocks, axis=1)


def _hg_kernel(hn_ref, w_ref, lb_ref, gnw_ref, lv_ref, y_ref, pj_ref, cum_ref, st_ref):
    tb = hn_ref.shape[0]
    L = CHUNK
    D = HG_HEAD_DIM
    W = HG_WIDTH

    @pl.when(pl.program_id(1) == 0)
    def _():
        st_ref[...] = jnp.zeros(st_ref.shape, F32)

    pj_ref[...] = _dot(hn_ref[...], w_ref[...])
    tri = _tril_ones_bf16(L)
    lb = lb_ref[...]
    lb_pos = lb > 0.0

    def chunk(c, carry):
        r0 = _chunk_start(c)
        rows = pl.ds(r0, L)
        fx = pj_ref[rows, W:2 * W]
        a = jnp.abs(fx)
        t = jnp.exp(-a)
        pos = fx >= 0.0
        one_t = 1.0 + t
        log_num = jnp.where(pos, jnp.log(1.0 + lb * t), jnp.where(lb_pos, jnp.log(t + lb), -a))
        log2_f = (log_num - jnp.log(one_t)) * LOG2E
        k = (1.0 - lb) * jnp.where(pos, t, 1.0) / one_t
        q = _silu(pj_ref[rows, 0:W])
        v_b = pj_ref[rows, 2 * W:3 * W].astype(BF16)
        cum = _cumsum_time(log2_f, tri)
        _store_col_blocks(cum_ref, 0, 0, cum)
        lv = lv_ref[...]
        q_b = q.astype(BF16)
        k_b = k.astype(BF16)
        attn = []
        for h in range(HG_HEADS):
            cols = slice(h * D, (h + 1) * D)
            attn.append(jnp.where(lv == HG_LEVELS, _dot_nt(q_b[:, cols], k_b[:, cols]), 0.0))
        for level in range(HG_LEVELS):
            e = jnp.exp2(-jnp.abs((cum - _hg_reference_rows(cum_ref, level)).astype(BF16)))
            qe = q_b * e
            ke = k_b * e
            for h in range(HG_HEADS):
                cols = slice(h * D, (h + 1) * D)
                attn[h] = jnp.where(lv == level, _dot_nt(qe[:, cols], ke[:, cols]), attn[h])
        cum_last = cum[L - 1:L, :]
        q_dec = (q * jnp.exp2(cum)).astype(BF16)
        k_end = k * jnp.exp2(cum_last - cum)
        dec_last = jnp.exp2(cum_last)
        gz = _silu(pj_ref[rows,3 * W:4 * W])
        for h in range(HG_HEADS):
            cols = slice(h * D, (h + 1) * D)
            st = st_ref[h]
            o = _dot(attn[h].astype(BF16), v_b[:, cols]) + _dot_nt(q_dec[:, cols], st.astype(BF16))
            v_t = pj_ref[rows,2 * W + h * D:2 * W + (h + 1) * D].T.astype(BF16)
            st_ref[h] = st * dec_last[:, cols] + _dot(v_t, k_end[:, cols].astype(BF16))
            on = o * lax.rsqrt(jnp.mean(o * o, axis=-1, keepdims=True) + EPS)
            y_ref[rows, cols] = (on * gnw_ref[:, cols] * gz[:, cols]).astype(BF16)
        return carry

    _for_chunks(tb // L, chunk)


def _merge_kernel(x_ref, hn_ref, wg_ref, ys_ref, ym_ref, yh_ref, wbs_ref, wbm_ref, wbh_ref, wo_ref,
                  nnw_ref, o_ref, *hn_next_ref, last_layer):
    x = x_ref[...]
    hn = hn_ref[...]
    d = x.shape[-1]
    merged = jax.nn.sigmoid(_dot(hn, wg_ref[:, 0:d])) * _dot(ys_ref[...], wbs_ref[...])
    merged = merged + jax.nn.sigmoid(_dot(hn, wg_ref[:, d:2 * d])) * _dot(ym_ref[...], wbm_ref[...])
    merged = merged + jax.nn.sigmoid(_dot(hn, wg_ref[:, 2 * d:3 * d])) * _dot(yh_ref[...], wbh_ref[...])
    out = x + _dot(merged.astype(BF16), wo_ref[...])
    if last_layer:
        o_ref[...] = _rmsnorm(out, nnw_ref[...])
    else:
        o_ref[...] = out
        hn_next_ref[0][...] = _rmsnorm(out, nnw_ref[...]).astype(BF16)


def _prenorm_kernel(x_ref, nw_ref, hn_ref):
    hn_ref[...] = _rmsnorm(x_ref[...], nw_ref[...]).astype(BF16)


def _segment_cast_kernel(w_ref, o_ref, *, keep_rows):
    w = w_ref[0]
    if keep_rows is not None:
        row = lax.broadcasted_iota(jnp.int32, w.shape, 0)
        w = jnp.where(row < keep_rows, w, 0.0)
    o_ref[...] = w.T.astype(BF16)


def _const_spec(shape):
    nd = len(shape)
    return pl.BlockSpec(shape, lambda b, t: (0,) * nd)


def _tok_spec(tb, width):
    return pl.BlockSpec((None, tb, width), lambda b, t: (b, t, 0))


def _layer_spec(block_shape, layer, col_block=0):
    nd = len(block_shape)
    return pl.BlockSpec((None,) + tuple(block_shape),
                        lambda b, t: (layer,) + (0,) * (nd - 1) + (col_block,))


def _with_specs(consts):
    pairs = [c if isinstance(c, tuple) else (c, _const_spec(c.shape)) for c in consts]
    return [a for a, _ in pairs], [sp for _, sp in pairs]


def _mixer_call(kernel_fn, name, x, consts, out_width, scratch_shapes, tb):
    bsz, seq, d = x.shape
    consts, const_specs = _with_specs(consts)
    return pl.pallas_call(
        kernel_fn,
        grid=(bsz, seq // tb),
        in_specs=[_tok_spec(tb, d)] + const_specs,
        out_specs=_tok_spec(tb, out_width),
        out_shape=jax.ShapeDtypeStruct((bsz, seq, out_width), BF16),
        scratch_shapes=scratch_shapes,
        compiler_params=pltpu.CompilerParams(
            dimension_semantics=("parallel", "arbitrary"),
            vmem_limit_bytes=VMEM_LIMIT_BYTES),
        name=name,
    )(x, *consts)


def _row(v):
    return v.reshape(1, -1).astype(F32)


SEGMENT_ROWS = 512


def _segment_cast(w_t, src, width, keep_rows=None):
    depth, _, d = w_t.shape
    rb = min(SEGMENT_ROWS, width)
    return pl.pallas_call(
        functools.partial(_segment_cast_kernel, keep_rows=keep_rows),
        grid=(depth, width // rb),
        in_specs=[pl.BlockSpec((pl.Element(1), pl.Element(rb), pl.Element(d)),
                               lambda l, j: (l, pl.multiple_of(src + j * rb, 16), 0))],
        out_specs=pl.BlockSpec((None, d, rb), lambda l, j: (l, 0, j)),
        out_shape=jax.ShapeDtypeStruct((depth, d, width), BF16),
        compiler_params=pltpu.CompilerParams(
            dimension_semantics=("parallel", "parallel"), vmem_limit_bytes=VMEM_LIMIT_BYTES),
        name='w_in_segment_cast',
    )(w_t)


def _split_w_in(w_in):
    w_t = jnp.swapaxes(w_in, 1, 2)
    o_dt = SSD_CONV_DIM
    o_z = o_dt + SSD_HEADS
    o_ml = o_z + SSD_WIDTH
    o_hg = o_ml + 3 * ML_WIDTH
    o_gate = o_hg + 4 * HG_WIDTH
    return dict(ssd=_segment_cast(w_t, 0, SSD_CONV_DIM),
                ssd_dt=_segment_cast(w_t, o_dt, LANES, keep_rows=SSD_HEADS),
                ssd_z=_segment_cast(w_t, o_z, SSD_WIDTH),
                ml=_segment_cast(w_t, o_ml, 3 * ML_WIDTH),
                hg=_segment_cast(w_t, o_hg, 4 * HG_WIDTH),
                gate=_segment_cast(w_t, o_gate, 3 * w_in.shape[1]))


def _layer(x, hn, layer, lb, p, big, next_norm_w, last_layer):
    bsz, seq, d = x.shape
    tb = min(TOKEN_BLOCK, seq)
    assert seq % tb == 0 and tb % CHUNK == 0
    w_ssd, w_ssd_dt, w_ssd_z, w_ml, w_hg, w_gate = [
        (big[k], _layer_spec(big[k].shape[1:], layer)) for k in ('ssd', 'ssd_dt', 'ssd_z', 'ml', 'hg', 'gate')]

    pad_h = (0, LANES - SSD_HEADS)
    ssd_consts = [
        w_ssd, w_ssd_dt, w_ssd_z, p['ssd_conv_w'].astype(F32), _row(p['ssd_conv_b']),
        _row(jnp.pad(p['ssd_dt_bias'], pad_h)), _row(jnp.pad(p['ssd_a_log'], pad_h)),
        _row(jnp.repeat(p['ssd_d'], SSD_HEAD_DIM)), _row(p['ssd_norm_w'])]
    y_ssd = _mixer_call(
        _ssd_kernel, 'ssd_mixer', hn, ssd_consts, SSD_WIDTH,
        [pltpu.VMEM((SSD_CONV_DIM // LANES, tb + SUBLANES, LANES), F32),
         pltpu.VMEM((tb, SSD_CONV_DIM), F32),
         pltpu.VMEM((tb, LANES), F32), pltpu.VMEM((tb, SSD_WIDTH), F32),
         pltpu.VMEM((SSD_GROUPS, SSD_STATE, SSD_WIDTH // SSD_GROUPS), F32)], tb)

    wqk = jnp.concatenate([p['ml_wq'], p['ml_wk']], axis=-1).astype(BF16)
    wv = p['ml_wv'].astype(BF16)
    wif = jnp.pad(p['ml_w_if'], ((0, 0), (0, LANES - 2 * ML_HEADS))).astype(BF16)
    bif = _row(jnp.pad(p['ml_b_if'], (0, LANES - 2 * ML_HEADS)))
    ml_consts = [w_ml, p['ml_conv_w'].astype(F32), _row(p['ml_conv_b']), wqk, wv, wif, bif,
                 _row(p['ml_norm_w']), _row(p['ml_skip'])]
    y_ml = _mixer_call(
        _ml_kernel, 'mlstm_mixer', hn, ml_consts, ML_WIDTH,
        [pltpu.VMEM((ML_WIDTH // LANES, tb + SUBLANES, LANES), F32), pltpu.VMEM((tb, ML_WIDTH), F32),
         pltpu.VMEM((tb, 2 * ML_WIDTH), F32), pltpu.VMEM((tb, 3 * ML_WIDTH), F32),
         pltpu.VMEM((tb, LANES), F32),
         pltpu.VMEM((ML_HEADS, ML_HEAD_DIM, ML_HEAD_DIM), F32),
         pltpu.VMEM((ML_HEADS, ML_HEAD_DIM, ML_HEAD_DIM), F32), pltpu.VMEM((SUBLANES, LANES), F32)], tb)

    hg_consts = [w_hg, _row(lb), _row(p['hg_norm_w']), jnp.asarray(_hg_level_table())]
    y_hg = _mixer_call(
        _hg_kernel, 'hgrn2_mixer', hn, hg_consts, HG_WIDTH,
        [pltpu.VMEM((tb, 4 * HG_WIDTH), F32), pltpu.VMEM((HG_WIDTH // LANES, CHUNK, LANES), F32),
         pltpu.VMEM((HG_HEADS, HG_HEAD_DIM, HG_HEAD_DIM), F32)], tb)

    tb = min(MERGE_BLOCK, seq)
    merge_consts_a, merge_specs_a = _with_specs([w_gate])
    merge_consts_b, merge_specs_b = _with_specs(
        [(big[k], _layer_spec(big[k].shape[1:], layer))
         for k in ('w_branch_ssd', 'w_branch_ml', 'w_branch_hg', 'w_out')] + [_row(next_norm_w)])
    out_specs = [_tok_spec(tb, d)]
    out_shape = [jax.ShapeDtypeStruct((bsz, seq, d), F32)]
    if not last_layer:
        out_specs.append(_tok_spec(tb, d))
        out_shape.append(jax.ShapeDtypeStruct((bsz, seq, d), BF16))
    outs = pl.pallas_call(
        functools.partial(_merge_kernel, last_layer=last_layer),
        grid=(bsz, seq // tb),
        in_specs=([_tok_spec(tb, d), _tok_spec(tb, d)] + merge_specs_a
                  + [_tok_spec(tb, SSD_WIDTH), _tok_spec(tb, ML_WIDTH), _tok_spec(tb, HG_WIDTH)]
                  + merge_specs_b),
        out_specs=out_specs,
        out_shape=out_shape,
        compiler_params=pltpu.CompilerParams(
            dimension_semantics=("parallel", "parallel"),
            vmem_limit_bytes=VMEM_LIMIT_BYTES),
        name='merge_out',
    )(x, hn, *merge_consts_a, y_ssd, y_ml, y_hg, *merge_consts_b)
    return (outs[0], None) if last_layer else (outs[0], outs[1])


def _prenorm(x, norm_w, tb):
    bsz, seq, d = x.shape
    return pl.pallas_call(
        _prenorm_kernel,
        grid=(bsz, seq // tb),
        in_specs=[_tok_spec(tb, d), _const_spec((1, d))],
        out_specs=_tok_spec(tb, d),
        out_shape=jax.ShapeDtypeStruct((bsz, seq, d), BF16),
        compiler_params=pltpu.CompilerParams(
            dimension_semantics=("parallel", "parallel"), vmem_limit_bytes=VMEM_LIMIT_BYTES),
        name='prenorm',
    )(x, _row(norm_w))


_LAYER_PARAMS = ('ssd_conv_w', 'ssd_conv_b', 'ssd_dt_bias', 'ssd_a_log', 'ssd_d',
                 'ssd_norm_w', 'ml_conv_w', 'ml_conv_b', 'ml_wq', 'ml_wk', 'ml_wv', 'ml_w_if',
                 'ml_b_if', 'ml_norm_w', 'ml_skip', 'hg_norm_w')


def kernel(x, norm_w, w_in, ssd_conv_w, ssd_conv_b, ssd_dt_bias, ssd_a_log, ssd_d, ssd_norm_w, ml_conv_w, ml_conv_b, ml_wq, ml_wk, ml_wv, ml_w_if, ml_b_if, ml_norm_w, ml_skip, hg_lower_bounds, hg_norm_w, w_branch_ssd, w_branch_ml, w_branch_hg, w_out, final_norm_w):
    stacked = dict(ssd_conv_w=ssd_conv_w, ssd_conv_b=ssd_conv_b,
                   ssd_dt_bias=ssd_dt_bias, ssd_a_log=ssd_a_log, ssd_d=ssd_d, ssd_norm_w=ssd_norm_w,
                   ml_conv_w=ml_conv_w, ml_conv_b=ml_conv_b, ml_wq=ml_wq, ml_wk=ml_wk, ml_wv=ml_wv,
                   ml_w_if=ml_w_if, ml_b_if=ml_b_if, ml_norm_w=ml_norm_w, ml_skip=ml_skip,
                   hg_norm_w=hg_norm_w)
    big = dict(**_split_w_in(w_in), w_branch_ssd=w_branch_ssd.astype(BF16),
               w_branch_ml=w_branch_ml.astype(BF16), w_branch_hg=w_branch_hg.astype(BF16),
               w_out=w_out.astype(BF16))
    depth = norm_w.shape[0]
    lbs = jnp.cumsum(jax.nn.softmax(hg_lower_bounds.astype(F32), axis=0), axis=0)
    lbs = lbs - lbs[0]
    hn = _prenorm(x, norm_w[0], min(MERGE_BLOCK, x.shape[1]))
    for l in range(depth):
        p = {k: stacked[k][l] for k in _LAYER_PARAMS}
        last = l == depth - 1
        x, hn = _layer(x, hn, l, lbs[l], p, big, final_norm_w if last else norm_w[l + 1], last)
    return x
```

```python
import functools

import numpy as np
import jax
import jax.numpy as jnp
from jax import lax
from jax.experimental import pallas as pl
from jax.experimental.pallas import tpu as pltpu

F32 = jnp.float32
BF16 = jnp.bfloat16

EPS = 1e-6
LOG2E = 1.4426950408889634
CONV_K = 4
LANES = 128
SUBLANES = 8
CHUNK = 128
TOKEN_BLOCK = 1024
MERGE_BLOCK = 512
PROJ_ROWS = 256
VMEM_LIMIT_BYTES = 56 * 1024 * 1024

SSD_HEAD_DIM = 64
SSD_HEADS = 16
SSD_GROUPS = 2
SSD_STATE = 128
SSD_WIDTH = SSD_HEADS * SSD_HEAD_DIM
SSD_CONV_DIM = SSD_WIDTH + 2 * SSD_GROUPS * SSD_STATE
ML_HEADS = 4
ML_HEAD_DIM = 128
ML_WIDTH = ML_HEADS * ML_HEAD_DIM
HG_HEADS = 4
HG_HEAD_DIM = 128
HG_WIDTH = HG_HEADS * HG_HEAD_DIM
HG_LEVELS = 7


def _dot(a, b):
    return jnp.dot(a, b, preferred_element_type=F32)


def _dot_nt(a, b):
    return lax.dot_general(a, b, (((1,), (1,)), ((), ())), preferred_element_type=F32)


def _rmsnorm(x, w):
    return x * lax.rsqrt(jnp.mean(x * x, axis=-1, keepdims=True) + EPS) * w


def _softplus(x):
    return jnp.maximum(x, 0.0) + jnp.log1p(jnp.exp(-jnp.abs(x)))


def _log_sigmoid(x):
    return jnp.minimum(x, 0.0) - jnp.log1p(jnp.exp(-jnp.abs(x)))


def _silu(x):
    return x * jax.nn.sigmoid(x)


def _tril_ones_bf16(n):
    r = lax.broadcasted_iota(jnp.int32, (n, n), 0)
    c = lax.broadcasted_iota(jnp.int32, (n, n), 1)
    return jnp.where(c <= r, 1.0, 0.0).astype(BF16)


def _cumsum_time(x, tri):
    hi = x.astype(BF16)
    r1 = x - hi.astype(F32)
    mid = r1.astype(BF16)
    lo = (r1 - mid.astype(F32)).astype(BF16)
    return _dot(tri, hi) + _dot(tri, mid) + _dot(tri, lo)


def _colb(x, j, n=LANES):
    return jnp.broadcast_to(x[:, j:j + 1], (x.shape[0], n))


def _rowb(x, j, m):
    return jnp.broadcast_to(x[j:j + 1, :], (m, x.shape[1]))


def _expand_heads(v):
    rows = v.shape[0]
    lane = lax.broadcasted_iota(jnp.int32, (rows, LANES), 1)
    parts = []
    for j in range(SSD_HEADS // 2):
        a = _colb(v, 2 * j)
        b = _colb(v, 2 * j + 1)
        parts.append(jnp.where(lane < SSD_HEAD_DIM, a, b))
    return jnp.concatenate(parts, axis=1)


def _chunk_start(c):
    return c * CHUNK if isinstance(c, int) else pl.multiple_of(c * CHUNK, CHUNK)


def _for_chunks(n, body):
    for c in range(n):
        body(c, 0)


def _store_col_blocks(dst_ref, first_block, row0, val):
    for i in range(val.shape[1] // LANES):
        dst_ref[first_block + i, pl.ds(row0, val.shape[0]), :] = val[:, i * LANES:(i + 1) * LANES]


def _causal_conv_silu(pre_ref, cw_ref, cb_ref, out_ref, r0, n, blocks=None):
    for blk in (range(pre_ref.shape[0]) if blocks is None else blocks):
        cols = slice(blk * LANES, (blk + 1) * LANES)
        acc = cb_ref[:, cols] + cw_ref[0:1, cols] * pre_ref[blk, pl.ds(r0 + SUBLANES - 3, n), :]
        for j in range(1, CONV_K):
            acc = acc + cw_ref[j:j + 1, cols] * pre_ref[blk, pl.ds(r0 + SUBLANES - 3 + j, n), :]
        out_ref[pl.ds(r0, n), cols] = _silu(acc)


def _conv_carry(pre_ref, tb):
    for blk in range(pre_ref.shape[0]):
        pre_ref[blk, 0:SUBLANES, :] = pre_ref[blk, pl.ds(tb, SUBLANES), :]


def _ssd_kernel(*refs, prenorm):
    refs = list(refs)
    src_ref = refs.pop(0)
    nw_ref = refs.pop(0) if prenorm else None
    w_ref, wdt_ref, wz_ref, cw_ref, cb_ref, dtb_ref, alog_ref, dskip_ref, gnw_ref, y_ref = refs[:10]
    del refs[:10]
    hn_out_ref = refs.pop(0) if prenorm else None
    pre_ref, xc_ref, dt_ref, z_ref, st_ref = refs
    tb = src_ref.shape[0]
    L = CHUNK

    @pl.when(pl.program_id(1) == 0)
    def _():
        pre_ref[:, 0:SUBLANES, :] = jnp.zeros((pre_ref.shape[0], SUBLANES, LANES), F32)
        st_ref[...] = jnp.zeros(st_ref.shape, F32)

    group = 4 * LANES
    for r0 in range(0, tb, PROJ_ROWS):
        prow = pl.ds(r0, PROJ_ROWS)
        if prenorm:
            hn = _rmsnorm(src_ref[prow, :], nw_ref[...]).astype(BF16)
            hn_out_ref[prow, :] = hn
        else:
            hn = src_ref[prow, :]
        conv_after = []
        for c0 in range(0, SSD_CONV_DIM, group):
            _store_col_blocks(pre_ref, c0 // LANES, SUBLANES + r0, _dot(hn, w_ref[:, c0:c0 + group]))
            for blocks in conv_after:
                _causal_conv_silu(pre_ref, cw_ref, cb_ref, xc_ref, r0, PROJ_ROWS, blocks)
            conv_after = [range(c0 // LANES, (c0 + group) // LANES)]
        dt_ref[prow, :] = _dot(hn, wdt_ref[...])
        half = SSD_WIDTH // 2
        z_ref[prow, 0:half] = _dot(hn, wz_ref[:, 0:half])
        _causal_conv_silu(pre_ref, cw_ref, cb_ref, xc_ref, r0, PROJ_ROWS, conv_after[0])
        z_ref[prow, half:] = _dot(hn, wz_ref[:, half:])
    _conv_carry(pre_ref, tb)

    lane_row = lax.broadcasted_iota(jnp.int32, (1, LANES), 1)
    a_row = jnp.where(lane_row < SSD_HEADS, -jnp.exp(alog_ref[...]) * LOG2E, 0.0)
    r_i = lax.broadcasted_iota(jnp.int32, (L, L), 0)
    c_i = lax.broadcasted_iota(jnp.int32, (L, L), 1)
    causal = c_i <= r_i
    lane = lax.broadcasted_iota(jnp.int32, (L, LANES), 1)
    lo_half = lane < SSD_HEAD_DIM
    tri = _tril_ones_bf16(L)
    gs = SSD_WIDTH // SSD_GROUPS
    pairs_per_group = SSD_HEADS // SSD_GROUPS // 2

    def chunk(c, carry):
        r0 = _chunk_start(c)
        rows = pl.ds(r0, L)
        xs = xc_ref[rows, 0:SSD_WIDTH]
        dt = _softplus(dt_ref[rows, :] + dtb_ref[...])
        cum = _cumsum_time(dt * a_row, tri)
        cum_last = cum[L - 1:L, :]
        r_t = (cum - jnp.log(dt) * LOG2E).T
        dtw_t = (dt * jnp.exp2(cum_last - cum)).T
        dec_last_x = _expand_heads(jnp.exp2(cum_last))
        xs_b = xs.astype(BF16)

        y_parts = []
        for g in range(SSD_GROUPS):
            bm = xc_ref[rows, SSD_WIDTH + g * SSD_STATE:SSD_WIDTH + (g + 1) * SSD_STATE]
            cm = xc_ref[rows, SSD_WIDTH + (SSD_GROUPS + g) * SSD_STATE:
                        SSD_WIDTH + (SSD_GROUPS + g + 1) * SSD_STATE]
            cm_b = cm.astype(BF16)
            cb = _dot_nt(cm_b, bm.astype(BF16))
            bm_t = bm.T
            h_t = st_ref[g]
            y_off = _dot(cm_b, h_t.astype(BF16))
            new_state = []
            for jp in range(pairs_per_group):
                pair = g * pairs_per_group + jp
                lanes = slice(pair * LANES, (pair + 1) * LANES)
                sc, bsc, dec = [], [], []
                for h in (2 * pair, 2 * pair + 1):
                    c_col = _colb(cum, h)
                    seg = c_col - _rowb(r_t, h, L)
                    sc.append((cb * jnp.exp2(jnp.where(causal, seg, -jnp.inf))).astype(BF16))
                    bsc.append((bm_t * _rowb(dtw_t, h, L)).astype(BF16))
                    dec.append(jnp.exp2(c_col))
                xp = xs_b[:, lanes]
                zero = jnp.zeros_like(xp)
                rhs = jnp.concatenate([jnp.where(lo_half, xp, zero), jnp.where(lo_half, zero, xp)], axis=0)
                y_diag = _dot(jnp.concatenate(sc, axis=1), rhs)
                local = _dot(jnp.concatenate(bsc, axis=1), rhs)
                y_parts.append(y_diag + y_off[:, jp * LANES:(jp + 1) * LANES]
                               * jnp.where(lo_half, dec[0], dec[1]))
                new_state.append(h_t[:, jp * LANES:(jp + 1) * LANES] * dec_last_x[:, lanes] + local)
            st_ref[g] = jnp.concatenate(new_state, axis=1)
        y = jnp.concatenate(y_parts, axis=1) + xs * dskip_ref[...]
        y = y * _silu(z_ref[rows, :])
        outs = []
        for g in range(SSD_GROUPS):
            yg = y[:, g * gs:(g + 1) * gs]
            outs.append(yg * lax.rsqrt(jnp.mean(yg * yg, axis=-1, keepdims=True) + EPS))
        y_ref[rows, :] = (jnp.concatenate(outs, axis=1) * gnw_ref[...]).astype(BF16)
        return carry

    _for_chunks(tb // L, chunk)


def _ml_kernel(hn_ref, w_ref, cw_ref, cb_ref, wqk_ref, wv_ref, wif_ref, bif_ref, lnw_ref,
               skip_ref, y_ref, pre_ref, mc_ref, oz_ref, qkv_ref, if_ref, ct_ref, nm_ref, m_ref):
    tb = hn_ref.shape[0]
    L = CHUNK
    D = ML_HEAD_DIM

    @pl.when(pl.program_id(1) == 0)
    def _():
        pre_ref[:, 0:SUBLANES, :] = jnp.zeros((pre_ref.shape[0], SUBLANES, LANES), F32)
        ct_ref[...] = jnp.zeros(ct_ref.shape, F32)
        nm_ref[...] = jnp.zeros(nm_ref.shape, F32)
        m_ref[...] = jnp.zeros(m_ref.shape, F32)

    for r0 in range(0, tb, PROJ_ROWS):
        prow = pl.ds(r0, PROJ_ROWS)
        hn = hn_ref[prow, :]
        _store_col_blocks(pre_ref, 0, SUBLANES + r0, _dot(hn, w_ref[:, 0:ML_WIDTH]))
        oz_ref[prow, :] = _dot(hn, w_ref[:, ML_WIDTH:])
        for h in range(ML_HEADS):
            v = _dot(pre_ref[h, pl.ds(SUBLANES + r0, PROJ_ROWS), :].astype(BF16), wv_ref[h])
            qkv_ref[prow, h * 3 * D + 2 * D:(h + 1) * 3 * D] = v
        _causal_conv_silu(pre_ref, cw_ref, cb_ref, mc_ref, r0, PROJ_ROWS)
        for h in range(ML_HEADS):
            qkv_ref[prow, h * 3 * D:h * 3 * D + 2 * D] = _dot(
                mc_ref[prow, h * D:(h + 1) * D].astype(BF16), wqk_ref[h])
        if_ref[prow, :] = _dot(qkv_ref[prow, :].astype(BF16), wif_ref[...]) + bif_ref[...]
    _conv_carry(pre_ref, tb)

    r_i = lax.broadcasted_iota(jnp.int32, (L, L), 0)
    c_i = lax.broadcasted_iota(jnp.int32, (L, L), 1)
    causal = c_i <= r_i
    tri = _tril_ones_bf16(L)
    scale = D ** -0.5

    ones_b = jnp.ones((L, D), BF16)
    heads = range(ML_HEADS)

    def chunk(c, carry):
        r0 = _chunk_start(c)
        rows = pl.ds(r0, L)
        if_pre = if_ref[rows, :]
        cum = _cumsum_time(_log_sigmoid(if_pre) * LOG2E, tri)
        cum_t = cum.T
        if_t = (if_pre * LOG2E).T
        q_b = [(qkv_ref[rows, h * 3 * D:h * 3 * D + D] * scale).astype(BF16) for h in heads]
        k = [qkv_ref[rows, h * 3 * D + D:h * 3 * D + 2 * D] for h in heads]
        v_aug = [jnp.concatenate([qkv_ref[rows, h * 3 * D + 2 * D:(h + 1) * 3 * D].astype(BF16), ones_b],
                                 axis=1) for h in heads]
        qk = [_dot_nt(q_b[h], k[h].astype(BF16)) for h in heads]
        k_t = [k[h].T for h in heads]
        m_in = [m_ref[h:h + 1, :] for h in heads]
        cn_in = [jnp.concatenate([ct_ref[h], nm_ref[h]], axis=1) for h in heads]
        c_col = [_colb(cum, ML_HEADS + h) for h in heads]
        a_row = [if_t[h:h + 1, :] - cum_t[ML_HEADS + h:ML_HEADS + h + 1, :] for h in heads]
        log_d = [jnp.where(causal, c_col[h] + a_row[h], -jnp.inf) for h in heads]
        log_inter = [c_col[h] + m_in[h] for h in heads]
        m_t = [jnp.maximum(log_inter[h], jnp.max(log_d[h], axis=1, keepdims=True)) for h in heads]
        s_b = [(qk[h] * jnp.exp2(log_d[h] - m_t[h])).astype(BF16) for h in heads]
        w_inter = [jnp.exp2(log_inter[h] - m_t[h]) for h in heads]
        intra = [_dot(s_b[h], v_aug[h]) for h in heads]
        inter = [_dot(q_b[h], cn_in[h].astype(BF16)) for h in heads]
        g_row = [c_col[h][L - 1:L, :] for h in heads]
        log_end = [g_row[h] + a_row[h] for h in heads]
        m_loc = [jnp.max(log_end[h], axis=1, keepdims=True) for h in heads]
        kw_b = [(k_t[h] * jnp.exp2(log_end[h] - m_loc[h])).astype(BF16) for h in heads]
        local = [_dot(kw_b[h], v_aug[h]) for h in heads]
        for h in heads:
            m_new = jnp.maximum(g_row[h] + m_in[h], m_loc[h])
            a_old = jnp.exp2(g_row[h] + m_in[h] - m_new)[:, 0:1]
            a_loc = jnp.exp2(m_loc[h] - m_new)[:, 0:1]
            cn_new = a_old * cn_in[h] + a_loc * local[h]
            ct_ref[h] = cn_new[:, 0:D]
            nm_ref[h] = cn_new[:, D:2 * D]
            m_ref[h:h + 1, :] = m_new
        hh = []
        for h in heads:
            both = intra[h] + jnp.concatenate([w_inter[h], w_inter[h]], axis=1) * inter[h]
            den = jnp.maximum(jnp.abs(both[:, D:2 * D]), jnp.exp2(-m_t[h]))
            hh.append(both[:, 0:D] / den)
        mu = [jnp.mean(hh[h], axis=-1, keepdims=True) for h in heads]
        xc = [hh[h] - mu[h] for h in heads]
        var = [jnp.mean(xc[h] * xc[h], axis=-1, keepdims=True) for h in heads]
        for h in heads:
            cols = slice(h * D, (h + 1) * D)
            ln = xc[h] * lax.rsqrt(var[h] + EPS) * lnw_ref[:, cols]
            o_gate = jax.nn.sigmoid(oz_ref[rows, h * D:(h + 1) * D])
            z = oz_ref[rows, ML_WIDTH + h * D:ML_WIDTH + (h + 1) * D]
            out = (ln * o_gate + skip_ref[:, cols] * mc_ref[rows, cols]) * _silu(z)
            y_ref[rows, cols] = out.astype(BF16)
        return carry

    _for_chunks(tb // L, chunk)


def _hg_level_table():
    l = np.arange(CHUNK)[:, None]
    s = np.arange(CHUNK)[None, :]
    x = l ^ s
    msb = np.floor(np.log2(np.maximum(x, 1))).astype(np.int32)
    return np.where(s < l, msb, np.where(s == l, HG_LEVELS, HG_LEVELS + 1)).astype(np.int32)


def _replicated_row(ref, blk, r):
    return ref[blk, pl.ds(r, SUBLANES, stride=0), :]


def _hg_reference_rows(cum_ref, level):
    b = 1 << level
    sub = lax.broadcasted_iota(jnp.int32, (SUBLANES, LANES), 0)
    col_blocks = []
    for blk in range(cum_ref.shape[0]):
        def row8(r, blk=blk):
            return _replicated_row(cum_ref, blk, r)
        pieces = []
        if 2 * b >= SUBLANES:
            for i in range(CHUNK // (2 * b)):
                pieces.extend([row8(i * 2 * b + b - 1)] * (2 * b // SUBLANES))
        else:
            for grp in range(CHUNK // SUBLANES):
                base = grp * SUBLANES
                acc = row8(base + b - 1)
                for i in range(1, SUBLANES // (2 * b)):
                    acc = jnp.where(sub >= i * 2 * b, row8(base + i * 2 * b + b - 1), acc)
                pieces.append(acc)
        col_blocks.append(jnp.concatenate(pieces, axis=0))
    return jnp.concatenate(col_blocks, axis=1)


def _hg_kernel(hn_ref, w_ref, lb_ref, gnw_ref, lv_ref, y_ref, pj_ref, cum_ref, st_ref):
    tb = hn_ref.shape[0]
    L = CHUNK
    D = HG_HEAD_DIM
    W = HG_WIDTH

    @pl.when(pl.program_id(1) == 0)
    def _():
        st_ref[...] = jnp.zeros(st_ref.shape, F32)

    pj_ref[...] = _dot(hn_ref[...], w_ref[...])
    tri = _tril_ones_bf16(L)
    lb = lb_ref[...]
    lb_pos = lb > 0.0

    def chunk(c, carry):
        r0 = _chunk_start(c)
        rows = pl.ds(r0, L)
        fx = pj_ref[rows, W:2 * W]
        a = jnp.abs(fx)
        t = jnp.exp(-a)
        pos = fx >= 0.0
        one_t = 1.0 + t
        log_num = jnp.where(pos, jnp.log(1.0 + lb * t), jnp.where(lb_pos, jnp.log(t + lb), -a))
        log2_f = (log_num - jnp.log(one_t)) * LOG2E
        k = (1.0 - lb) * jnp.where(pos, t, 1.0) / one_t
        q = _silu(pj_ref[rows, 0:W])
        v_b = pj_ref[rows, 2 * W:3 * W].astype(BF16)
        cum = _cumsum_time(log2_f, tri)
        _store_col_blocks(cum_ref, 0, 0, cum)
        lv = lv_ref[...]
        q_b = q.astype(BF16)
        k_b = k.astype(BF16)
        attn = []
        for h in range(HG_HEADS):
            cols = slice(h * D, (h + 1) * D)
            attn.append(jnp.where(lv == HG_LEVELS, _dot_nt(q_b[:, cols], k_b[:, cols]), 0.0))
        for level in range(HG_LEVELS):
            e = jnp.exp2(-jnp.abs((cum - _hg_reference_rows(cum_ref, level)).astype(BF16)))
            qe = q_b * e
            ke = k_b * e
            for h in range(HG_HEADS):
                cols = slice(h * D, (h + 1) * D)
                attn[h] = jnp.where(lv == level, _dot_nt(qe[:, cols], ke[:, cols]), attn[h])
        cum_last = cum[L - 1:L, :]
        q_dec = (q * jnp.exp2(cum)).astype(BF16)
        k_end = k * jnp.exp2(cum_last - cum)
        dec_last = jnp.exp2(cum_last)
        gz = _silu(pj_ref[rows,3 * W:4 * W])
        for h in range(HG_HEADS):
            cols = slice(h * D, (h + 1) * D)
            st = st_ref[h]
            o = _dot(attn[h].astype(BF16), v_b[:, cols]) + _dot_nt(q_dec[:, cols], st.astype(BF16))
            v_t = pj_ref[rows,2 * W + h * D:2 * W + (h + 1) * D].T.astype(BF16)
            st_ref[h] = st * dec_last[:, cols] + _dot(v_t, k_end[:, cols].astype(BF16))
            on = o * lax.rsqrt(jnp.mean(o * o, axis=-1, keepdims=True) + EPS)
            y_ref[rows, cols] = (on * gnw_ref[:, cols] * gz[:, cols]).astype(BF16)
        return carry

    _for_chunks(tb // L, chunk)


def _merge_kernel(x_ref, hn_ref, wg_ref, ys_ref, ym_ref, yh_ref, wbs_ref, wbm_ref, wbh_ref, wo_ref,
                  nnw_ref, o_ref, *hn_next_ref, last_layer):
    x = x_ref[...]
    hn = hn_ref[...]
    d = x.shape[-1]
    merged = jax.nn.sigmoid(_dot(hn, wg_ref[:, 0:d])) * _dot(ys_ref[...], wbs_ref[...])
    merged = merged + jax.nn.sigmoid(_dot(hn, wg_ref[:, d:2 * d])) * _dot(ym_ref[...], wbm_ref[...])
    merged = merged + jax.nn.sigmoid(_dot(hn, wg_ref[:, 2 * d:3 * d])) * _dot(yh_ref[...], wbh_ref[...])
    out = x + _dot(merged.astype(BF16), wo_ref[...])
    if last_layer:
        o_ref[...] = _rmsnorm(out, nnw_ref[...])
    else:
        o_ref[...] = out
        hn_next_ref[0][...] = _rmsnorm(out, nnw_ref[...]).astype(BF16)


def _segment_cast_kernel(w_ref, o_ref, *, keep_rows):
    w = w_ref[0]
    if keep_rows is not None:
        row = lax.broadcasted_iota(jnp.int32, w.shape, 0)
        w = jnp.where(row < keep_rows, w, 0.0)
    o_ref[...] = w.T.astype(BF16)


def _const_spec(shape):
    nd = len(shape)
    return pl.BlockSpec(shape, lambda b, t: (0,) * nd)


def _tok_spec(tb, width):
    return pl.BlockSpec((None, tb, width), lambda b, t: (b, t, 0))


def _layer_spec(block_shape, layer, col_block=0):
    nd = len(block_shape)
    return pl.BlockSpec((None,) + tuple(block_shape),
                        lambda b, t: (layer,) + (0,) * (nd - 1) + (col_block,))


def _with_specs(consts):
    pairs = [c if isinstance(c, tuple) else (c, _const_spec(c.shape)) for c in consts]
    return [a for a, _ in pairs], [sp for _, sp in pairs]


def _mixer_call(kernel_fn, name, x, consts, out_width, scratch_shapes, tb, extra_out_width=None):
    bsz, seq, d = x.shape
    consts, const_specs = _with_specs(consts)
    widths = [out_width] + ([] if extra_out_width is None else [extra_out_width])
    return pl.pallas_call(
        kernel_fn,
        grid=(bsz, seq // tb),
        in_specs=[_tok_spec(tb, d)] + const_specs,
        out_specs=[_tok_spec(tb, w) for w in widths],
        out_shape=[jax.ShapeDtypeStruct((bsz, seq, w), BF16) for w in widths],
        scratch_shapes=scratch_shapes,
        compiler_params=pltpu.CompilerParams(
            dimension_semantics=("parallel", "arbitrary"),
            vmem_limit_bytes=VMEM_LIMIT_BYTES),
        name=name,
    )(x, *consts)


def _row(v):
    return v.reshape(1, -1).astype(F32)


SEGMENT_ROWS = 512


def _segment_cast(w_t, src, width, keep_rows=None):
    depth, _, d = w_t.shape
    rb = min(SEGMENT_ROWS, width)
    return pl.pallas_call(
        functools.partial(_segment_cast_kernel, keep_rows=keep_rows),
        grid=(depth, width // rb),
        in_specs=[pl.BlockSpec((pl.Element(1), pl.Element(rb), pl.Element(d)),
                               lambda l, j: (l, pl.multiple_of(src + j * rb, 16), 0))],
        out_specs=pl.BlockSpec((None, d, rb), lambda l, j: (l, 0, j)),
        out_shape=jax.ShapeDtypeStruct((depth, d, width), BF16),
        compiler_params=pltpu.CompilerParams(
            dimension_semantics=("parallel", "parallel"), vmem_limit_bytes=VMEM_LIMIT_BYTES),
        name='w_in_segment_cast',
    )(w_t)


def _split_w_in(w_in):
    w_t = jnp.swapaxes(w_in, 1, 2)
    o_dt = SSD_CONV_DIM
    o_z = o_dt + SSD_HEADS
    o_ml = o_z + SSD_WIDTH
    o_hg = o_ml + 3 * ML_WIDTH
    o_gate = o_hg + 4 * HG_WIDTH
    return dict(ssd=_segment_cast(w_t, 0, SSD_CONV_DIM),
                ssd_dt=_segment_cast(w_t, o_dt, LANES, keep_rows=SSD_HEADS),
                ssd_z=_segment_cast(w_t, o_z, SSD_WIDTH),
                ml=_segment_cast(w_t, o_ml, 3 * ML_WIDTH),
                hg=_segment_cast(w_t, o_hg, 4 * HG_WIDTH),
                gate=_segment_cast(w_t, o_gate, 3 * w_in.shape[1]))


def _layer(x, hn, layer, prm, last_layer):
    bsz, seq, d = x.shape
    tb = min(TOKEN_BLOCK, seq)
    assert seq % tb == 0 and tb % CHUNK == 0

    def lw(*names):
        return [(prm[k], _layer_spec(prm[k].shape[1:], layer)) for k in names]

    ssd_consts = lw('ssd', 'ssd_dt', 'ssd_z', 'ssd_conv_w', 'ssd_conv_b', 'ssd_dt_bias', 'ssd_a_log',
                    'ssd_d', 'ssd_norm_w')
    ssd_scratch = [pltpu.VMEM((SSD_CONV_DIM // LANES, tb + SUBLANES, LANES), F32),
                   pltpu.VMEM((tb, SSD_CONV_DIM), F32),
                   pltpu.VMEM((tb, LANES), F32), pltpu.VMEM((tb, SSD_WIDTH), F32),
                   pltpu.VMEM((SSD_GROUPS, SSD_STATE, SSD_WIDTH // SSD_GROUPS), F32)]
    if hn is None:
        y_ssd, hn = _mixer_call(functools.partial(_ssd_kernel, prenorm=True), 'ssd_mixer', x,
                                lw('norm_w') + ssd_consts, SSD_WIDTH, ssd_scratch, tb, extra_out_width=d)
    else:
        y_ssd, = _mixer_call(functools.partial(_ssd_kernel, prenorm=False), 'ssd_mixer', hn,
                             ssd_consts, SSD_WIDTH, ssd_scratch, tb)

    ml_consts = lw('ml', 'ml_conv_w', 'ml_conv_b', 'ml_wqk', 'ml_wv', 'ml_w_if', 'ml_b_if',
                   'ml_norm_w', 'ml_skip')
    y_ml, = _mixer_call(
        _ml_kernel, 'mlstm_mixer', hn, ml_consts, ML_WIDTH,
        [pltpu.VMEM((ML_WIDTH // LANES, tb + SUBLANES, LANES), F32), pltpu.VMEM((tb, ML_WIDTH), F32),
         pltpu.VMEM((tb, 2 * ML_WIDTH), F32), pltpu.VMEM((tb, 3 * ML_WIDTH), F32),
         pltpu.VMEM((tb, LANES), F32),
         pltpu.VMEM((ML_HEADS, ML_HEAD_DIM, ML_HEAD_DIM), F32),
         pltpu.VMEM((ML_HEADS, ML_HEAD_DIM, ML_HEAD_DIM), F32), pltpu.VMEM((SUBLANES, LANES), F32)], tb)

    hg_consts = lw('hg', 'hg_lb', 'hg_norm_w') + [jnp.asarray(_hg_level_table())]
    y_hg, = _mixer_call(
        _hg_kernel, 'hgrn2_mixer', hn, hg_consts, HG_WIDTH,
        [pltpu.VMEM((tb, 4 * HG_WIDTH), F32), pltpu.VMEM((HG_WIDTH // LANES, CHUNK, LANES), F32),
         pltpu.VMEM((HG_HEADS, HG_HEAD_DIM, HG_HEAD_DIM), F32)], tb)

    tb = min(MERGE_BLOCK, seq)
    next_norm = (prm['final_norm_w'] if last_layer
                 else (prm['norm_w'], _layer_spec(prm['norm_w'].shape[1:], layer + 1)))
    merge_consts_a, merge_specs_a = _with_specs(lw('gate'))
    merge_consts_b, merge_specs_b = _with_specs(
        lw('w_branch_ssd', 'w_branch_ml', 'w_branch_hg', 'w_out') + [next_norm])
    out_specs = [_tok_spec(tb, d)]
    out_shape = [jax.ShapeDtypeStruct((bsz, seq, d), F32)]
    if not last_layer:
        out_specs.append(_tok_spec(tb, d))
        out_shape.append(jax.ShapeDtypeStruct((bsz, seq, d), BF16))
    outs = pl.pallas_call(
        functools.partial(_merge_kernel, last_layer=last_layer),
        grid=(bsz, seq // tb),
        in_specs=([_tok_spec(tb, d), _tok_spec(tb, d)] + merge_specs_a
                  + [_tok_spec(tb, SSD_WIDTH), _tok_spec(tb, ML_WIDTH), _tok_spec(tb, HG_WIDTH)]
                  + merge_specs_b),
        out_specs=out_specs,
        out_shape=out_shape,
        compiler_params=pltpu.CompilerParams(
            dimension_semantics=("parallel", "parallel"),
            vmem_limit_bytes=VMEM_LIMIT_BYTES),
        name='merge_out',
    )(x, hn, *merge_consts_a, y_ssd, y_ml, y_hg, *merge_consts_b)
    return (outs[0], None) if last_layer else (outs[0], outs[1])


def _rows(v):
    return v.reshape(v.shape[0], 1, -1).astype(F32)


def kernel(x, norm_w, w_in, ssd_conv_w, ssd_conv_b, ssd_dt_bias, ssd_a_log, ssd_d, ssd_norm_w, ml_conv_w, ml_conv_b, ml_wq, ml_wk, ml_wv, ml_w_if, ml_b_if, ml_norm_w, ml_skip, hg_lower_bounds, hg_norm_w, w_branch_ssd, w_branch_ml, w_branch_hg, w_out, final_norm_w):
    depth = norm_w.shape[0]
    pad_h = ((0, 0), (0, LANES - SSD_HEADS))
    pad_if = LANES - 2 * ML_HEADS
    lbs = jnp.cumsum(jax.nn.softmax(hg_lower_bounds.astype(F32), axis=0), axis=0)
    lbs = lbs - lbs[0]
    prm = dict(
        **_split_w_in(w_in),
        w_branch_ssd=w_branch_ssd.astype(BF16), w_branch_ml=w_branch_ml.astype(BF16),
        w_branch_hg=w_branch_hg.astype(BF16), w_out=w_out.astype(BF16),
        norm_w=_rows(norm_w), final_norm_w=_row(final_norm_w),
        ssd_conv_w=ssd_conv_w.astype(F32), ssd_conv_b=_rows(ssd_conv_b),
        ssd_dt_bias=_rows(jnp.pad(ssd_dt_bias, pad_h)), ssd_a_log=_rows(jnp.pad(ssd_a_log, pad_h)),
        ssd_d=_rows(jnp.repeat(ssd_d, SSD_HEAD_DIM, axis=1)), ssd_norm_w=_rows(ssd_norm_w),
        ml_conv_w=ml_conv_w.astype(F32), ml_conv_b=_rows(ml_conv_b),
        ml_wqk=jnp.concatenate([ml_wq, ml_wk], axis=-1).astype(BF16),
        ml_wv=ml_wv.astype(BF16),
        ml_w_if=jnp.pad(ml_w_if, ((0, 0), (0, 0), (0, pad_if))).astype(BF16),
        ml_b_if=_rows(jnp.pad(ml_b_if, ((0, 0), (0, pad_if)))),
        ml_norm_w=_rows(ml_norm_w), ml_skip=_rows(ml_skip),
        hg_lb=_rows(lbs), hg_norm_w=_rows(hg_norm_w))
    hn = None
    for l in range(depth):
        x, hn = _layer(x, hn, l, prm, last_layer=(l == depth - 1))
    return x
```

```python
import functools

import numpy as np
import jax
import jax.numpy as jnp
from jax import lax
from jax.experimental import pallas as pl
from jax.experimental.pallas import tpu as pltpu

F32 = jnp.float32
BF16 = jnp.bfloat16

EPS = 1e-6
LOG2E = 1.4426950408889634
CONV_K = 4
LANES = 128
SUBLANES = 8
CHUNK = 128
TOKEN_BLOCK = 1024
MERGE_BLOCK = 1024
PROJ_ROWS = 256
VMEM_LIMIT_BYTES = 56 * 1024 * 1024

SSD_HEAD_DIM = 64
SSD_HEADS = 16
SSD_GROUPS = 2
SSD_STATE = 128
SSD_WIDTH = SSD_HEADS * SSD_HEAD_DIM
SSD_CONV_DIM = SSD_WIDTH + 2 * SSD_GROUPS * SSD_STATE
ML_HEADS = 4
ML_HEAD_DIM = 128
ML_WIDTH = ML_HEADS * ML_HEAD_DIM
HG_HEADS = 4
HG_HEAD_DIM = 128
HG_WIDTH = HG_HEADS * HG_HEAD_DIM
HG_LEVELS = 7


def _dot(a, b):
    return jnp.dot(a, b, preferred_element_type=F32)


def _dot_nt(a, b):
    return lax.dot_general(a, b, (((1,), (1,)), ((), ())), preferred_element_type=F32)


def _rmsnorm(x, w):
    return x * lax.rsqrt(jnp.mean(x * x, axis=-1, keepdims=True) + EPS) * w


def _softplus(x):
    return jnp.maximum(x, 0.0) + jnp.log1p(jnp.exp(-jnp.abs(x)))


def _log_sigmoid(x):
    return jnp.minimum(x, 0.0) - jnp.log1p(jnp.exp(-jnp.abs(x)))


def _silu(x):
    return x * jax.nn.sigmoid(x)


def _tril_ones_bf16(n):
    r = lax.broadcasted_iota(jnp.int32, (n, n), 0)
    c = lax.broadcasted_iota(jnp.int32, (n, n), 1)
    return jnp.where(c <= r, 1.0, 0.0).astype(BF16)


def _cumsum_time(x, tri):
    hi = x.astype(BF16)
    r1 = x - hi.astype(F32)
    mid = r1.astype(BF16)
    lo = (r1 - mid.astype(F32)).astype(BF16)
    return _dot(tri, hi) + _dot(tri, mid) + _dot(tri, lo)


def _colb(x, j, n=LANES):
    return jnp.broadcast_to(x[:, j:j + 1], (x.shape[0], n))


def _rowb(x, j, m):
    return jnp.broadcast_to(x[j:j + 1, :], (m, x.shape[1]))


def _expand_heads(v):
    rows = v.shape[0]
    lane = lax.broadcasted_iota(jnp.int32, (rows, LANES), 1)
    parts = []
    for j in range(SSD_HEADS // 2):
        a = _colb(v, 2 * j)
        b = _colb(v, 2 * j + 1)
        parts.append(jnp.where(lane < SSD_HEAD_DIM, a, b))
    return jnp.concatenate(parts, axis=1)


def _chunk_start(c):
    return c * CHUNK if isinstance(c, int) else pl.multiple_of(c * CHUNK, CHUNK)


def _for_chunks(n, body):
    for c in range(n):
        body(c, 0)


def _store_col_blocks(dst_ref, first_block, row0, val):
    for i in range(val.shape[1] // LANES):
        dst_ref[first_block + i, pl.ds(row0, val.shape[0]), :] = val[:, i * LANES:(i + 1) * LANES]


def _causal_conv_silu(pre_ref, cw_ref, cb_ref, out_ref, r0, n, blocks=None):
    for blk in (range(pre_ref.shape[0]) if blocks is None else blocks):
        cols = slice(blk * LANES, (blk + 1) * LANES)
        acc = cb_ref[:, cols] + cw_ref[0:1, cols] * pre_ref[blk, pl.ds(r0 + SUBLANES - 3, n), :]
        for j in range(1, CONV_K):
            acc = acc + cw_ref[j:j + 1, cols] * pre_ref[blk, pl.ds(r0 + SUBLANES - 3 + j, n), :]
        out_ref[pl.ds(r0, n), cols] = _silu(acc)


def _conv_carry(pre_ref, tb):
    for blk in range(pre_ref.shape[0]):
        pre_ref[blk, 0:SUBLANES, :] = pre_ref[blk, pl.ds(tb, SUBLANES), :]


def _ssd_kernel(*refs, prenorm):
    refs = list(refs)
    src_ref = refs.pop(0)
    nw_ref = refs.pop(0) if prenorm else None
    w_ref, wdt_ref, wz_ref, cw_ref, cb_ref, dtb_ref, alog_ref, dskip_ref, gnw_ref, y_ref = refs[:10]
    del refs[:10]
    hn_out_ref = refs.pop(0) if prenorm else None
    pre_ref, xc_ref, dt_ref, z_ref, st_ref = refs
    tb = src_ref.shape[0]
    L = CHUNK

    @pl.when(pl.program_id(1) == 0)
    def _():
        pre_ref[:, 0:SUBLANES, :] = jnp.zeros((pre_ref.shape[0], SUBLANES, LANES), F32)
        st_ref[...] = jnp.zeros(st_ref.shape, F32)

    group = 4 * LANES
    for r0 in range(0, tb, PROJ_ROWS):
        prow = pl.ds(r0, PROJ_ROWS)
        if prenorm:
            hn = _rmsnorm(src_ref[prow, :], nw_ref[...]).astype(BF16)
            hn_out_ref[prow, :] = hn
        else:
            hn = src_ref[prow, :]
        conv_after = []
        for c0 in range(0, SSD_CONV_DIM, group):
            _store_col_blocks(pre_ref, c0 // LANES, SUBLANES + r0, _dot(hn, w_ref[:, c0:c0 + group]))
            for blocks in conv_after:
                _causal_conv_silu(pre_ref, cw_ref, cb_ref, xc_ref, r0, PROJ_ROWS, blocks)
            conv_after = [range(c0 // LANES, (c0 + group) // LANES)]
        dt_ref[prow, :] = _dot(hn, wdt_ref[...])
        half = SSD_WIDTH // 2
        z_ref[prow, 0:half] = _dot(hn, wz_ref[:, 0:half])
        _causal_conv_silu(pre_ref, cw_ref, cb_ref, xc_ref, r0, PROJ_ROWS, conv_after[0])
        z_ref[prow, half:] = _dot(hn, wz_ref[:, half:])
    _conv_carry(pre_ref, tb)

    lane_row = lax.broadcasted_iota(jnp.int32, (1, LANES), 1)
    a_row = jnp.where(lane_row < SSD_HEADS, -jnp.exp(alog_ref[...]) * LOG2E, 0.0)
    r_i = lax.broadcasted_iota(jnp.int32, (L, L), 0)
    c_i = lax.broadcasted_iota(jnp.int32, (L, L), 1)
    causal = c_i <= r_i
    lane = lax.broadcasted_iota(jnp.int32, (L, LANES), 1)
    lo_half = lane < SSD_HEAD_DIM
    tri = _tril_ones_bf16(L)
    gs = SSD_WIDTH // SSD_GROUPS
    pairs_per_group = SSD_HEADS // SSD_GROUPS // 2

    def chunk(c, carry):
        r0 = _chunk_start(c)
        rows = pl.ds(r0, L)
        xs = xc_ref[rows, 0:SSD_WIDTH]
        dt = _softplus(dt_ref[rows, :] + dtb_ref[...])
        cum = _cumsum_time(dt * a_row, tri)
        cum_last = cum[L - 1:L, :]
        r_t = (cum - jnp.log(dt) * LOG2E).T
        dtw_t = (dt * jnp.exp2(cum_last - cum)).T
        dec_last_x = _expand_heads(jnp.exp2(cum_last))
        xs_b = xs.astype(BF16)

        y_parts = []
        for g in range(SSD_GROUPS):
            bm = xc_ref[rows, SSD_WIDTH + g * SSD_STATE:SSD_WIDTH + (g + 1) * SSD_STATE]
            cm = xc_ref[rows, SSD_WIDTH + (SSD_GROUPS + g) * SSD_STATE:
                        SSD_WIDTH + (SSD_GROUPS + g + 1) * SSD_STATE]
            cm_b = cm.astype(BF16)
            cb = _dot_nt(cm_b, bm.astype(BF16))
            bm_t = bm.T
            h_t = st_ref[g]
            y_off = _dot(cm_b, h_t.astype(BF16))
            new_state = []
            for jp in range(pairs_per_group):
                pair = g * pairs_per_group + jp
                lanes = slice(pair * LANES, (pair + 1) * LANES)
                sc, bsc, dec = [], [], []
                for h in (2 * pair, 2 * pair + 1):
                    c_col = _colb(cum, h)
                    seg = c_col - _rowb(r_t, h, L)
                    sc.append((cb * jnp.exp2(jnp.where(causal, seg, -jnp.inf))).astype(BF16))
                    bsc.append((bm_t * _rowb(dtw_t, h, L)).astype(BF16))
                    dec.append(jnp.exp2(c_col))
                xp = xs_b[:, lanes]
                zero = jnp.zeros_like(xp)
                rhs = jnp.concatenate([jnp.where(lo_half, xp, zero), jnp.where(lo_half, zero, xp)], axis=0)
                y_diag = _dot(jnp.concatenate(sc, axis=1), rhs)
                local = _dot(jnp.concatenate(bsc, axis=1), rhs)
                y_parts.append(y_diag + y_off[:, jp * LANES:(jp + 1) * LANES]
                               * jnp.where(lo_half, dec[0], dec[1]))
                new_state.append(h_t[:, jp * LANES:(jp + 1) * LANES] * dec_last_x[:, lanes] + local)
            st_ref[g] = jnp.concatenate(new_state, axis=1)
        y = jnp.concatenate(y_parts, axis=1) + xs * dskip_ref[...]
        y = y * _silu(z_ref[rows, :])
        outs = []
        for g in range(SSD_GROUPS):
            yg = y[:, g * gs:(g + 1) * gs]
            outs.append(yg * lax.rsqrt(jnp.mean(yg * yg, axis=-1, keepdims=True) + EPS))
        y_ref[rows, :] = (jnp.concatenate(outs, axis=1) * gnw_ref[...]).astype(BF16)
        return carry

    _for_chunks(tb // L, chunk)


def _ml_kernel(hn_ref, w_ref, cw_ref, cb_ref, wqk_ref, wv_ref, wif_ref, bif_ref, lnw_ref,
               skip_ref, y_ref, pre_ref, mc_ref, oz_ref, qkv_ref, if_ref, ct_ref, nm_ref, m_ref):
    tb = hn_ref.shape[0]
    L = CHUNK
    D = ML_HEAD_DIM

    @pl.when(pl.program_id(1) == 0)
    def _():
        pre_ref[:, 0:SUBLANES, :] = jnp.zeros((pre_ref.shape[0], SUBLANES, LANES), F32)
        ct_ref[...] = jnp.zeros(ct_ref.shape, F32)
        nm_ref[...] = jnp.zeros(nm_ref.shape, F32)
        m_ref[...] = jnp.zeros(m_ref.shape, F32)

    for r0 in range(0, tb, PROJ_ROWS):
        prow = pl.ds(r0, PROJ_ROWS)
        hn = hn_ref[prow, :]
        _store_col_blocks(pre_ref, 0, SUBLANES + r0, _dot(hn, w_ref[:, 0:ML_WIDTH]))
        oz_ref[prow, :] = _dot(hn, w_ref[:, ML_WIDTH:])
        for h in range(ML_HEADS):
            v = _dot(pre_ref[h, pl.ds(SUBLANES + r0, PROJ_ROWS), :].astype(BF16), wv_ref[h])
            qkv_ref[prow, h * 3 * D + 2 * D:(h + 1) * 3 * D] = v
        _causal_conv_silu(pre_ref, cw_ref, cb_ref, mc_ref, r0, PROJ_ROWS)
        for h in range(ML_HEADS):
            qkv_ref[prow, h * 3 * D:h * 3 * D + 2 * D] = _dot(
                mc_ref[prow, h * D:(h + 1) * D].astype(BF16), wqk_ref[h])
        if_ref[prow, :] = _dot(qkv_ref[prow, :].astype(BF16), wif_ref[...]) + bif_ref[...]
    _conv_carry(pre_ref, tb)

    r_i = lax.broadcasted_iota(jnp.int32, (L, L), 0)
    c_i = lax.broadcasted_iota(jnp.int32, (L, L), 1)
    causal = c_i <= r_i
    tri = _tril_ones_bf16(L)
    scale = D ** -0.5

    ones_b = jnp.ones((L, D), BF16)
    heads = range(ML_HEADS)

    def chunk(c, carry):
        r0 = _chunk_start(c)
        rows = pl.ds(r0, L)
        if_pre = if_ref[rows, :]
        cum = _cumsum_time(_log_sigmoid(if_pre) * LOG2E, tri)
        cum_t = cum.T
        if_t = (if_pre * LOG2E).T
        q_b = [(qkv_ref[rows, h * 3 * D:h * 3 * D + D] * scale).astype(BF16) for h in heads]
        k = [qkv_ref[rows, h * 3 * D + D:h * 3 * D + 2 * D] for h in heads]
        v_aug = [jnp.concatenate([qkv_ref[rows, h * 3 * D + 2 * D:(h + 1) * 3 * D].astype(BF16), ones_b],
                                 axis=1) for h in heads]
        qk = [_dot_nt(q_b[h], k[h].astype(BF16)) for h in heads]
        k_t = [k[h].T for h in heads]
        m_in = [m_ref[h:h + 1, :] for h in heads]
        cn_in = [jnp.concatenate([ct_ref[h], nm_ref[h]], axis=1) for h in heads]
        c_col = [_colb(cum, ML_HEADS + h) for h in heads]
        a_row = [if_t[h:h + 1, :] - cum_t[ML_HEADS + h:ML_HEADS + h + 1, :] for h in heads]
        log_d = [jnp.where(causal, c_col[h] + a_row[h], -jnp.inf) for h in heads]
        log_inter = [c_col[h] + m_in[h] for h in heads]
        m_t = [jnp.maximum(log_inter[h], jnp.max(log_d[h], axis=1, keepdims=True)) for h in heads]
        s_b = [(qk[h] * jnp.exp2(log_d[h] - m_t[h])).astype(BF16) for h in heads]
        w_inter = [jnp.exp2(log_inter[h] - m_t[h]) for h in heads]
        intra = [_dot(s_b[h], v_aug[h]) for h in heads]
        inter = [_dot(q_b[h], cn_in[h].astype(BF16)) for h in heads]
        g_row = [c_col[h][L - 1:L, :] for h in heads]
        log_end = [g_row[h] + a_row[h] for h in heads]
        m_loc = [jnp.max(log_end[h], axis=1, keepdims=True) for h in heads]
        kw_b = [(k_t[h] * jnp.exp2(log_end[h] - m_loc[h])).astype(BF16) for h in heads]
        local = [_dot(kw_b[h], v_aug[h]) for h in heads]
        for h in heads:
            m_new = jnp.maximum(g_row[h] + m_in[h], m_loc[h])
            a_old = jnp.exp2(g_row[h] + m_in[h] - m_new)[:, 0:1]
            a_loc = jnp.exp2(m_loc[h] - m_new)[:, 0:1]
            cn_new = a_old * cn_in[h] + a_loc * local[h]
            ct_ref[h] = cn_new[:, 0:D]
            nm_ref[h] = cn_new[:, D:2 * D]
            m_ref[h:h + 1, :] = m_new
        hh = []
        for h in heads:
            both = intra[h] + jnp.concatenate([w_inter[h], w_inter[h]], axis=1) * inter[h]
            den = jnp.maximum(jnp.abs(both[:, D:2 * D]), jnp.exp2(-m_t[h]))
            hh.append(both[:, 0:D] / den)
        mu = [jnp.mean(hh[h], axis=-1, keepdims=True) for h in heads]
        xc = [hh[h] - mu[h] for h in heads]
        var = [jnp.mean(xc[h] * xc[h], axis=-1, keepdims=True) for h in heads]
        for h in heads:
            cols = slice(h * D, (h + 1) * D)
            ln = xc[h] * lax.rsqrt(var[h] + EPS) * lnw_ref[:, cols]
            o_gate = jax.nn.sigmoid(oz_ref[rows, h * D:(h + 1) * D])
            z = oz_ref[rows, ML_WIDTH + h * D:ML_WIDTH + (h + 1) * D]
            out = (ln * o_gate + skip_ref[:, cols] * mc_ref[rows, cols]) * _silu(z)
            y_ref[rows, cols] = out.astype(BF16)
        return carry

    _for_chunks(tb // L, chunk)


def _hg_level_table():
    l = np.arange(CHUNK)[:, None]
    s = np.arange(CHUNK)[None, :]
    x = l ^ s
    msb = np.floor(np.log2(np.maximum(x, 1))).astype(np.int32)
    return np.where(s < l, msb, np.where(s == l, HG_LEVELS, HG_LEVELS + 1)).astype(np.int32)


def _replicated_row(ref, blk, r):
    return ref[blk, pl.ds(r, SUBLANES, stride=0), :]


def _hg_reference_rows(cum_ref, level):
    b = 1 << level
    sub = lax.broadcasted_iota(jnp.int32, (SUBLANES, LANES), 0)
    col_blocks = []
    for blk in range(cum_ref.shape[0]):
        def row8(r, blk=blk):
            return _replicated_row(cum_ref, blk, r)
        pieces = []
        if 2 * b >= SUBLANES:
            for i in range(CHUNK // (2 * b)):
                pieces.extend([row8(i * 2 * b + b - 1)] * (2 * b // SUBLANES))
        else:
            for grp in range(CHUNK // SUBLANES):
                base = grp * SUBLANES
                acc = row8(base + b - 1)
                for i in range(1, SUBLANES // (2 * b)):
                    acc = jnp.where(sub >= i * 2 * b, row8(base + i * 2 * b + b - 1), acc)
                pieces.append(acc)
        col_blocks.append(jnp.concatenate(pieces, axis=0))
    return jnp.concatenate(col_blocks, axis=1)


def _hg_kernel(hn_ref, w_ref, lb_ref, gnw_ref, lv_ref, y_ref, pj_ref, cum_ref, st_ref):
    tb = hn_ref.shape[0]
    L = CHUNK
    D = HG_HEAD_DIM
    W = HG_WIDTH

    @pl.when(pl.program_id(1) == 0)
    def _():
        st_ref[...] = jnp.zeros(st_ref.shape, F32)

    pj_ref[...] = _dot(hn_ref[...], w_ref[...])
    tri = _tril_ones_bf16(L)
    lb = lb_ref[...]
    lb_pos = lb > 0.0

    def chunk(c, carry):
        r0 = _chunk_start(c)
        rows = pl.ds(r0, L)
        fx = pj_ref[rows, W:2 * W]
        a = jnp.abs(fx)
        t = jnp.exp(-a)
        pos = fx >= 0.0
        one_t = 1.0 + t
        log_num = jnp.where(pos, jnp.log(1.0 + lb * t), jnp.where(lb_pos, jnp.log(t + lb), -a))
        log2_f = (log_num - jnp.log(one_t)) * LOG2E
        k = (1.0 - lb) * jnp.where(pos, t, 1.0) / one_t
        q = _silu(pj_ref[rows, 0:W])
        v_b = pj_ref[rows, 2 * W:3 * W].astype(BF16)
        cum = _cumsum_time(log2_f, tri)
        _store_col_blocks(cum_ref, 0, 0, cum)
        lv = lv_ref[...]
        q_b = q.astype(BF16)
        k_b = k.astype(BF16)
        attn = []
        for h in range(HG_HEADS):
            cols = slice(h * D, (h + 1) * D)
            attn.append(jnp.where(lv == HG_LEVELS, _dot_nt(q_b[:, cols], k_b[:, cols]), 0.0))
        for level in range(HG_LEVELS):
            e = jnp.exp2(-jnp.abs((cum - _hg_reference_rows(cum_ref, level)).astype(BF16)))
            qe = q_b * e
            ke = k_b * e
            for h in range(HG_HEADS):
                cols = slice(h * D, (h + 1) * D)
                attn[h] = jnp.where(lv == level, _dot_nt(qe[:, cols], ke[:, cols]), attn[h])
        cum_last = cum[L - 1:L, :]
        q_dec = (q * jnp.exp2(cum)).astype(BF16)
        k_end = k * jnp.exp2(cum_last - cum)
        dec_last = jnp.exp2(cum_last)
        gz = _silu(pj_ref[rows,3 * W:4 * W])
        for h in range(HG_HEADS):
            cols = slice(h * D, (h + 1) * D)
            st = st_ref[h]
            o = _dot(attn[h].astype(BF16), v_b[:, cols]) + _dot_nt(q_dec[:, cols], st.astype(BF16))
            v_t = pj_ref[rows,2 * W + h * D:2 * W + (h + 1) * D].T.astype(BF16)
            st_ref[h] = st * dec_last[:, cols] + _dot(v_t, k_end[:, cols].astype(BF16))
            on = o * lax.rsqrt(jnp.mean(o * o, axis=-1, keepdims=True) + EPS)
            y_ref[rows, cols] = (on * gnw_ref[:, cols] * gz[:, cols]).astype(BF16)
        return carry

    _for_chunks(tb // L, chunk)


def _merge_kernel(x_ref, hn_ref, wg_ref, ys_ref, ym_ref, yh_ref, wbs_ref, wbm_ref, wbh_ref, wo_ref,
                  nnw_ref, o_ref, *hn_next_ref, last_layer):
    x = x_ref[...]
    hn = hn_ref[...]
    d = x.shape[-1]
    merged = jax.nn.sigmoid(_dot(hn, wg_ref[:, 0:d])) * _dot(ys_ref[...], wbs_ref[...])
    merged = merged + jax.nn.sigmoid(_dot(hn, wg_ref[:, d:2 * d])) * _dot(ym_ref[...], wbm_ref[...])
    merged = merged + jax.nn.sigmoid(_dot(hn, wg_ref[:, 2 * d:3 * d])) * _dot(yh_ref[...], wbh_ref[...])
    out = x + _dot(merged.astype(BF16), wo_ref[...])
    if last_layer:
        o_ref[...] = _rmsnorm(out, nnw_ref[...])
    else:
        o_ref[...] = out
        hn_next_ref[0][...] = _rmsnorm(out, nnw_ref[...]).astype(BF16)


def _segment_cast_kernel(w_ref, o_ref, *, keep_rows):
    w = w_ref[0]
    if keep_rows is not None:
        row = lax.broadcasted_iota(jnp.int32, w.shape, 0)
        w = jnp.where(row < keep_rows, w, 0.0)
    o_ref[...] = w.T.astype(BF16)


def _const_spec(shape):
    nd = len(shape)
    return pl.BlockSpec(shape, lambda b, t: (0,) * nd)


def _tok_spec(tb, width):
    return pl.BlockSpec((None, tb, width), lambda b, t: (b, t, 0))


def _layer_spec(block_shape, layer, col_block=0):
    nd = len(block_shape)
    return pl.BlockSpec((None,) + tuple(block_shape),
                        lambda b, t: (layer,) + (0,) * (nd - 1) + (col_block,))


def _with_specs(consts):
    pairs = [c if isinstance(c, tuple) else (c, _const_spec(c.shape)) for c in consts]
    return [a for a, _ in pairs], [sp for _, sp in pairs]


def _mixer_call(kernel_fn, name, x, consts, out_width, scratch_shapes, tb, extra_out_width=None):
    bsz, seq, d = x.shape
    consts, const_specs = _with_specs(consts)
    widths = [out_width] + ([] if extra_out_width is None else [extra_out_width])
    return pl.pallas_call(
        kernel_fn,
        grid=(bsz, seq // tb),
        in_specs=[_tok_spec(tb, d)] + const_specs,
        out_specs=[_tok_spec(tb, w) for w in widths],
        out_shape=[jax.ShapeDtypeStruct((bsz, seq, w), BF16) for w in widths],
        scratch_shapes=scratch_shapes,
        compiler_params=pltpu.CompilerParams(
            dimension_semantics=("parallel", "arbitrary"),
            vmem_limit_bytes=VMEM_LIMIT_BYTES),
        name=name,
    )(x, *consts)


def _row(v):
    return v.reshape(1, -1).astype(F32)


SEGMENT_ROWS = 1536


def _segment_cast(w_t, src, width, keep_rows=None):
    depth, _, d = w_t.shape
    rb = max(r for r in range(LANES, min(SEGMENT_ROWS, width) + 1, LANES) if width % r == 0)
    return pl.pallas_call(
        functools.partial(_segment_cast_kernel, keep_rows=keep_rows),
        grid=(depth, width // rb),
        in_specs=[pl.BlockSpec((pl.Element(1), pl.Element(rb), pl.Element(d)),
                               lambda l, j: (l, pl.multiple_of(src + j * rb, 16), 0))],
        out_specs=pl.BlockSpec((None, d, rb), lambda l, j: (l, 0, j)),
        out_shape=jax.ShapeDtypeStruct((depth, d, width), BF16),
        compiler_params=pltpu.CompilerParams(
            dimension_semantics=("parallel", "parallel"), vmem_limit_bytes=VMEM_LIMIT_BYTES),
        name='w_in_segment_cast',
    )(w_t)


def _split_w_in(w_in):
    w_t = jnp.swapaxes(w_in, 1, 2)
    o_dt = SSD_CONV_DIM
    o_z = o_dt + SSD_HEADS
    o_ml = o_z + SSD_WIDTH
    o_hg = o_ml + 3 * ML_WIDTH
    o_gate = o_hg + 4 * HG_WIDTH
    return dict(ssd=_segment_cast(w_t, 0, SSD_CONV_DIM),
                ssd_dt=_segment_cast(w_t, o_dt, LANES, keep_rows=SSD_HEADS),
                ssd_z=_segment_cast(w_t, o_z, SSD_WIDTH),
                ml=_segment_cast(w_t, o_ml, 3 * ML_WIDTH),
                hg=_segment_cast(w_t, o_hg, 4 * HG_WIDTH),
                gate=_segment_cast(w_t, o_gate, 3 * w_in.shape[1]))


def _layer(x, hn, layer, prm, last_layer):
    bsz, seq, d = x.shape
    tb = min(TOKEN_BLOCK, seq)
    assert seq % tb == 0 and tb % CHUNK == 0

    def lw(*names):
        return [(prm[k], _layer_spec(prm[k].shape[1:], layer)) for k in names]

    ssd_consts = lw('ssd', 'ssd_dt', 'ssd_z', 'ssd_conv_w', 'ssd_conv_b', 'ssd_dt_bias', 'ssd_a_log',
                    'ssd_d', 'ssd_norm_w')
    ssd_scratch = [pltpu.VMEM((SSD_CONV_DIM // LANES, tb + SUBLANES, LANES), F32),
                   pltpu.VMEM((tb, SSD_CONV_DIM), F32),
                   pltpu.VMEM((tb, LANES), F32), pltpu.VMEM((tb, SSD_WIDTH), F32),
                   pltpu.VMEM((SSD_GROUPS, SSD_STATE, SSD_WIDTH // SSD_GROUPS), F32)]
    if hn is None:
        y_ssd, hn = _mixer_call(functools.partial(_ssd_kernel, prenorm=True), 'ssd_mixer', x,
                                lw('norm_w') + ssd_consts, SSD_WIDTH, ssd_scratch, tb, extra_out_width=d)
    else:
        y_ssd, = _mixer_call(functools.partial(_ssd_kernel, prenorm=False), 'ssd_mixer', hn,
                             ssd_consts, SSD_WIDTH, ssd_scratch, tb)

    ml_consts = lw('ml', 'ml_conv_w', 'ml_conv_b', 'ml_wqk', 'ml_wv', 'ml_w_if', 'ml_b_if',
                   'ml_norm_w', 'ml_skip')
    y_ml, = _mixer_call(
        _ml_kernel, 'mlstm_mixer', hn, ml_consts, ML_WIDTH,
        [pltpu.VMEM((ML_WIDTH // LANES, tb + SUBLANES, LANES), F32), pltpu.VMEM((tb, ML_WIDTH), F32),
         pltpu.VMEM((tb, 2 * ML_WIDTH), F32), pltpu.VMEM((tb, 3 * ML_WIDTH), F32),
         pltpu.VMEM((tb, LANES), F32),
         pltpu.VMEM((ML_HEADS, ML_HEAD_DIM, ML_HEAD_DIM), F32),
         pltpu.VMEM((ML_HEADS, ML_HEAD_DIM, ML_HEAD_DIM), F32), pltpu.VMEM((SUBLANES, LANES), F32)], tb)

    hg_consts = lw('hg', 'hg_lb', 'hg_norm_w') + [jnp.asarray(_hg_level_table())]
    y_hg, = _mixer_call(
        _hg_kernel, 'hgrn2_mixer', hn, hg_consts, HG_WIDTH,
        [pltpu.VMEM((tb, 4 * HG_WIDTH), F32), pltpu.VMEM((HG_WIDTH // LANES, CHUNK, LANES), F32),
         pltpu.VMEM((HG_HEADS, HG_HEAD_DIM, HG_HEAD_DIM), F32)], tb)

    tb = min(MERGE_BLOCK, seq)
    next_norm = (prm['final_norm_w'] if last_layer
                 else (prm['norm_w'], _layer_spec(prm['norm_w'].shape[1:], layer + 1)))
    merge_consts_a, merge_specs_a = _with_specs(lw('gate'))
    merge_consts_b, merge_specs_b = _with_specs(
        lw('w_branch_ssd', 'w_branch_ml', 'w_branch_hg', 'w_out') + [next_norm])
    out_specs = [_tok_spec(tb, d)]
    out_shape = [jax.ShapeDtypeStruct((bsz, seq, d), F32)]
    if not last_layer:
        out_specs.append(_tok_spec(tb, d))
        out_shape.append(jax.ShapeDtypeStruct((bsz, seq, d), BF16))
    outs = pl.pallas_call(
        functools.partial(_merge_kernel, last_layer=last_layer),
        grid=(bsz, seq // tb),
        in_specs=([_tok_spec(tb, d), _tok_spec(tb, d)] + merge_specs_a
                  + [_tok_spec(tb, SSD_WIDTH), _tok_spec(tb, ML_WIDTH), _tok_spec(tb, HG_WIDTH)]
                  + merge_specs_b),
        out_specs=out_specs,
        out_shape=out_shape,
        compiler_params=pltpu.CompilerParams(
            dimension_semantics=("parallel", "parallel"),
            vmem_limit_bytes=VMEM_LIMIT_BYTES),
        name='merge_out',
    )(x, hn, *merge_consts_a, y_ssd, y_ml, y_hg, *merge_consts_b)
    return (outs[0], None) if last_layer else (outs[0], outs[1])


def _rows(v):
    return v.reshape(v.shape[0], 1, -1).astype(F32)


def kernel(x, norm_w, w_in, ssd_conv_w, ssd_conv_b, ssd_dt_bias, ssd_a_log, ssd_d, ssd_norm_w, ml_conv_w, ml_conv_b, ml_wq, ml_wk, ml_wv, ml_w_if, ml_b_if, ml_norm_w, ml_skip, hg_lower_bounds, hg_norm_w, w_branch_ssd, w_branch_ml, w_branch_hg, w_out, final_norm_w):
    depth = norm_w.shape[0]
    pad_h = ((0, 0), (0, LANES - SSD_HEADS))
    pad_if = LANES - 2 * ML_HEADS
    lbs = jnp.cumsum(jax.nn.softmax(hg_lower_bounds.astype(F32), axis=0), axis=0)
    lbs = lbs - lbs[0]
    prm = dict(
        **_split_w_in(w_in),
        w_branch_ssd=w_branch_ssd.astype(BF16), w_branch_ml=w_branch_ml.astype(BF16),
        w_branch_hg=w_branch_hg.astype(BF16), w_out=w_out.astype(BF16),
        norm_w=_rows(norm_w), final_norm_w=_row(final_norm_w),
        ssd_conv_w=ssd_conv_w.astype(F32), ssd_conv_b=_rows(ssd_conv_b),
        ssd_dt_bias=_rows(jnp.pad(ssd_dt_bias, pad_h)), ssd_a_log=_rows(jnp.pad(ssd_a_log, pad_h)),
        ssd_d=_rows(jnp.repeat(ssd_d, SSD_HEAD_DIM, axis=1)), ssd_norm_w=_rows(ssd_norm_w),
        ml_conv_w=ml_conv_w.astype(F32), ml_conv_b=_rows(ml_conv_b),
        ml_wqk=jnp.concatenate([ml_wq, ml_wk], axis=-1).astype(BF16),
        ml_wv=ml_wv.astype(BF16),
        ml_w_if=jnp.pad(ml_w_if, ((0, 0), (0, 0), (0, pad_if))).astype(BF16),
        ml_b_if=_rows(jnp.pad(ml_b_if, ((0, 0), (0, pad_if)))),
        ml_norm_w=_rows(ml_norm_w), ml_skip=_rows(ml_skip),
        hg_lb=_rows(lbs), hg_norm_w=_rows(hg_norm_w))
    hn = None
    for l in range(depth):
        x, hn = _layer(x, hn, l, prm, last_layer=(l == depth - 1))
    return x
```

```python
import functools

import numpy as np
import jax
import jax.numpy as jnp
from jax import lax
from jax.experimental import pallas as pl
from jax.experimental.pallas import tpu as pltpu

F32 = jnp.float32
BF16 = jnp.bfloat16

EPS = 1e-6
LOG2E = 1.4426950408889634
CONV_K = 4
LANES = 128
SUBLANES = 8
CHUNK = 128
TOKEN_BLOCK = 1024
MERGE_BLOCK = 1024
PROJ_ROWS = 256
ML_CHUNK_GROUP = 8
VMEM_LIMIT_BYTES = 56 * 1024 * 1024

SSD_HEAD_DIM = 64
SSD_HEADS = 16
SSD_GROUPS = 2
SSD_STATE = 128
SSD_WIDTH = SSD_HEADS * SSD_HEAD_DIM
SSD_CONV_DIM = SSD_WIDTH + 2 * SSD_GROUPS * SSD_STATE
ML_HEADS = 4
ML_HEAD_DIM = 128
ML_WIDTH = ML_HEADS * ML_HEAD_DIM
HG_HEADS = 4
HG_HEAD_DIM = 128
HG_WIDTH = HG_HEADS * HG_HEAD_DIM
HG_LEVELS = 7


def _dot(a, b):
    return jnp.dot(a, b, preferred_element_type=F32)


def _dot_nt(a, b):
    return lax.dot_general(a, b, (((1,), (1,)), ((), ())), preferred_element_type=F32)


def _rmsnorm(x, w):
    return x * lax.rsqrt(jnp.mean(x * x, axis=-1, keepdims=True) + EPS) * w


def _softplus(x):
    return jnp.maximum(x, 0.0) + jnp.log1p(jnp.exp(-jnp.abs(x)))


def _log_sigmoid(x):
    return jnp.minimum(x, 0.0) - jnp.log1p(jnp.exp(-jnp.abs(x)))


def _silu(x):
    return x * jax.nn.sigmoid(x)


def _tril_ones_bf16(n):
    r = lax.broadcasted_iota(jnp.int32, (n, n), 0)
    c = lax.broadcasted_iota(jnp.int32, (n, n), 1)
    return jnp.where(c <= r, 1.0, 0.0).astype(BF16)


def _cumsum_time(x, tri):
    hi = x.astype(BF16)
    r1 = x - hi.astype(F32)
    mid = r1.astype(BF16)
    lo = (r1 - mid.astype(F32)).astype(BF16)
    return _dot(tri, hi) + _dot(tri, mid) + _dot(tri, lo)


def _colb(x, j, n=LANES):
    return jnp.broadcast_to(x[:, j:j + 1], (x.shape[0], n))


def _rowb(x, j, m):
    return jnp.broadcast_to(x[j:j + 1, :], (m, x.shape[1]))


def _expand_heads(v):
    rows = v.shape[0]
    lane = lax.broadcasted_iota(jnp.int32, (rows, LANES), 1)
    parts = []
    for j in range(SSD_HEADS // 2):
        a = _colb(v, 2 * j)
        b = _colb(v, 2 * j + 1)
        parts.append(jnp.where(lane < SSD_HEAD_DIM, a, b))
    return jnp.concatenate(parts, axis=1)


def _chunk_start(c):
    return c * CHUNK if isinstance(c, int) else pl.multiple_of(c * CHUNK, CHUNK)


def _for_chunks(n, body):
    for c in range(n):
        body(c, 0)


def _store_col_blocks(dst_ref, first_block, row0, val):
    for i in range(val.shape[1] // LANES):
        dst_ref[first_block + i, pl.ds(row0, val.shape[0]), :] = val[:, i * LANES:(i + 1) * LANES]


def _causal_conv_silu(pre_ref, cw_ref, cb_ref, out_ref, r0, n, blocks=None):
    for blk in (range(pre_ref.shape[0]) if blocks is None else blocks):
        cols = slice(blk * LANES, (blk + 1) * LANES)
        acc = cb_ref[:, cols] + cw_ref[0:1, cols] * pre_ref[blk, pl.ds(r0 + SUBLANES - 3, n), :]
        for j in range(1, CONV_K):
            acc = acc + cw_ref[j:j + 1, cols] * pre_ref[blk, pl.ds(r0 + SUBLANES - 3 + j, n), :]
        out_ref[pl.ds(r0, n), cols] = _silu(acc)


def _conv_carry(pre_ref, tb):
    for blk in range(pre_ref.shape[0]):
        pre_ref[blk, 0:SUBLANES, :] = pre_ref[blk, pl.ds(tb, SUBLANES), :]


def _ssd_kernel(*refs, prenorm):
    refs = list(refs)
    src_ref = refs.pop(0)
    nw_ref = refs.pop(0) if prenorm else None
    w_ref, wdt_ref, wz_ref, cw_ref, cb_ref, dtb_ref, alog_ref, dskip_ref, gnw_ref, y_ref = refs[:10]
    del refs[:10]
    hn_out_ref = refs.pop(0) if prenorm else None
    pre_ref, xc_ref, dt_ref, z_ref, st_ref = refs
    tb = src_ref.shape[0]
    L = CHUNK

    @pl.when(pl.program_id(1) == 0)
    def _():
        pre_ref[:, 0:SUBLANES, :] = jnp.zeros((pre_ref.shape[0], SUBLANES, LANES), F32)
        st_ref[...] = jnp.zeros(st_ref.shape, F32)

    group = 4 * LANES
    for r0 in range(0, tb, PROJ_ROWS):
        prow = pl.ds(r0, PROJ_ROWS)
        if prenorm:
            hn = _rmsnorm(src_ref[prow, :], nw_ref[...]).astype(BF16)
            hn_out_ref[prow, :] = hn
        else:
            hn = src_ref[prow, :]
        conv_after = []
        for c0 in range(0, SSD_CONV_DIM, group):
            _store_col_blocks(pre_ref, c0 // LANES, SUBLANES + r0, _dot(hn, w_ref[:, c0:c0 + group]))
            for blocks in conv_after:
                _causal_conv_silu(pre_ref, cw_ref, cb_ref, xc_ref, r0, PROJ_ROWS, blocks)
            conv_after = [range(c0 // LANES, (c0 + group) // LANES)]
        dt_ref[prow, :] = _dot(hn, wdt_ref[...])
        half = SSD_WIDTH // 2
        z_ref[prow, 0:half] = _dot(hn, wz_ref[:, 0:half])
        _causal_conv_silu(pre_ref, cw_ref, cb_ref, xc_ref, r0, PROJ_ROWS, conv_after[0])
        z_ref[prow, half:] = _dot(hn, wz_ref[:, half:])
    _conv_carry(pre_ref, tb)

    lane_row = lax.broadcasted_iota(jnp.int32, (1, LANES), 1)
    a_row = jnp.where(lane_row < SSD_HEADS, -jnp.exp(alog_ref[...]) * LOG2E, 0.0)
    r_i = lax.broadcasted_iota(jnp.int32, (L, L), 0)
    c_i = lax.broadcasted_iota(jnp.int32, (L, L), 1)
    causal = c_i <= r_i
    lane = lax.broadcasted_iota(jnp.int32, (L, LANES), 1)
    lo_half = lane < SSD_HEAD_DIM
    tri = _tril_ones_bf16(L)
    gs = SSD_WIDTH // SSD_GROUPS
    pairs_per_group = SSD_HEADS // SSD_GROUPS // 2

    def chunk(c, carry):
        r0 = _chunk_start(c)
        rows = pl.ds(r0, L)
        xs = xc_ref[rows, 0:SSD_WIDTH]
        dt = _softplus(dt_ref[rows, :] + dtb_ref[...])
        cum = _cumsum_time(dt * a_row, tri)
        cum_last = cum[L - 1:L, :]
        r_t = (cum - jnp.log(dt) * LOG2E).T
        dtw_t = (dt * jnp.exp2(cum_last - cum)).T
        dec_last_x = _expand_heads(jnp.exp2(cum_last))
        xs_b = xs.astype(BF16)

        y_parts = []
        for g in range(SSD_GROUPS):
            bm = xc_ref[rows, SSD_WIDTH + g * SSD_STATE:SSD_WIDTH + (g + 1) * SSD_STATE]
            cm = xc_ref[rows, SSD_WIDTH + (SSD_GROUPS + g) * SSD_STATE:
                        SSD_WIDTH + (SSD_GROUPS + g + 1) * SSD_STATE]
            cm_b = cm.astype(BF16)
            cb = _dot_nt(cm_b, bm.astype(BF16))
            bm_t = bm.T
            h_t = st_ref[g]
            y_off = _dot(cm_b, h_t.astype(BF16))
            new_state = []
            for jp in range(pairs_per_group):
                pair = g * pairs_per_group + jp
                lanes = slice(pair * LANES, (pair + 1) * LANES)
                sc, bsc, dec = [], [], []
                for h in (2 * pair, 2 * pair + 1):
                    c_col = _colb(cum, h)
                    seg = c_col - _rowb(r_t, h, L)
                    sc.append((cb * jnp.exp2(jnp.where(causal, seg, -jnp.inf))).astype(BF16))
                    bsc.append((bm_t * _rowb(dtw_t, h, L)).astype(BF16))
                    dec.append(jnp.exp2(c_col))
                xp = xs_b[:, lanes]
                zero = jnp.zeros_like(xp)
                rhs = jnp.concatenate([jnp.where(lo_half, xp, zero), jnp.where(lo_half, zero, xp)], axis=0)
                y_diag = _dot(jnp.concatenate(sc, axis=1), rhs)
                local = _dot(jnp.concatenate(bsc, axis=1), rhs)
                y_parts.append(y_diag + y_off[:, jp * LANES:(jp + 1) * LANES]
                               * jnp.where(lo_half, dec[0], dec[1]))
                new_state.append(h_t[:, jp * LANES:(jp + 1) * LANES] * dec_last_x[:, lanes] + local)
            st_ref[g] = jnp.concatenate(new_state, axis=1)
        y = jnp.concatenate(y_parts, axis=1) + xs * dskip_ref[...]
        y = y * _silu(z_ref[rows, :])
        outs = []
        for g in range(SSD_GROUPS):
            yg = y[:, g * gs:(g + 1) * gs]
            outs.append(yg * lax.rsqrt(jnp.mean(yg * yg, axis=-1, keepdims=True) + EPS))
        y_ref[rows, :] = (jnp.concatenate(outs, axis=1) * gnw_ref[...]).astype(BF16)
        return carry

    _for_chunks(tb // L, chunk)


def _ml_kernel(hn_ref, w_ref, cw_ref, cb_ref, wqk_ref, wv_ref, wif_ref, bif_ref, lnw_ref,
               skip_ref, y_ref, pre_ref, mc_ref, oz_ref, qkv_ref, if_ref, ct_ref, nm_ref, m_ref):
    tb = hn_ref.shape[0]
    L = CHUNK
    D = ML_HEAD_DIM

    @pl.when(pl.program_id(1) == 0)
    def _():
        pre_ref[:, 0:SUBLANES, :] = jnp.zeros((pre_ref.shape[0], SUBLANES, LANES), F32)
        ct_ref[...] = jnp.zeros(ct_ref.shape, F32)
        nm_ref[...] = jnp.zeros(nm_ref.shape, F32)
        m_ref[...] = jnp.zeros(m_ref.shape, F32)

    for r0 in range(0, tb, PROJ_ROWS):
        prow = pl.ds(r0, PROJ_ROWS)
        hn = hn_ref[prow, :]
        _store_col_blocks(pre_ref, 0, SUBLANES + r0, _dot(hn, w_ref[:, 0:ML_WIDTH]))
        oz_ref[prow, :] = _dot(hn, w_ref[:, ML_WIDTH:])
        for h in range(ML_HEADS):
            v = _dot(pre_ref[h, pl.ds(SUBLANES + r0, PROJ_ROWS), :].astype(BF16), wv_ref[h])
            qkv_ref[prow, h * 3 * D + 2 * D:(h + 1) * 3 * D] = v
        _causal_conv_silu(pre_ref, cw_ref, cb_ref, mc_ref, r0, PROJ_ROWS)
        for h in range(ML_HEADS):
            qkv_ref[prow, h * 3 * D:h * 3 * D + 2 * D] = _dot(
                mc_ref[prow, h * D:(h + 1) * D].astype(BF16), wqk_ref[h])
        if_ref[prow, :] = _dot(qkv_ref[prow, :].astype(BF16), wif_ref[...]) + bif_ref[...]
    _conv_carry(pre_ref, tb)

    r_i = lax.broadcasted_iota(jnp.int32, (L, L), 0)
    c_i = lax.broadcasted_iota(jnp.int32, (L, L), 1)
    causal = c_i <= r_i
    tri = _tril_ones_bf16(L)
    scale = D ** -0.5

    ones_b = jnp.ones((L, D), BF16)
    heads = range(ML_HEADS)
    group = range(ML_CHUNK_GROUP)
    units = [(ci, h) for ci in group for h in heads]

    def chunk_group(c0):
        rows = [pl.ds((c0 + ci) * L, L) for ci in group]
        if_pre = [if_ref[rows[ci], :] for ci in group]
        cum = [_cumsum_time(_log_sigmoid(if_pre[ci]) * LOG2E, tri) for ci in group]
        cum_t = [cum[ci].T for ci in group]
        if_t = [(if_pre[ci] * LOG2E).T for ci in group]
        q_b = {(ci, h): (qkv_ref[rows[ci], h * 3 * D:h * 3 * D + D] * scale).astype(BF16) for ci, h in units}
        k = {(ci, h): qkv_ref[rows[ci], h * 3 * D + D:h * 3 * D + 2 * D] for ci, h in units}
        v_aug = {(ci, h): jnp.concatenate(
            [qkv_ref[rows[ci], h * 3 * D + 2 * D:(h + 1) * 3 * D].astype(BF16), ones_b], axis=1)
            for ci, h in units}
        qk = {u: _dot_nt(q_b[u], k[u].astype(BF16)) for u in units}
        k_t = {u: k[u].T for u in units}
        c_col = {(ci, h): _colb(cum[ci], ML_HEADS + h) for ci, h in units}
        a_row = {(ci, h): if_t[ci][h:h + 1, :] - cum_t[ci][ML_HEADS + h:ML_HEADS + h + 1, :]
                 for ci, h in units}
        log_d = {u: jnp.where(causal, c_col[u] + a_row[u], -jnp.inf) for u in units}
        row_max = {u: jnp.max(log_d[u], axis=1, keepdims=True) for u in units}
        g_row = {u: c_col[u][L - 1:L, :] for u in units}
        log_end = {u: g_row[u] + a_row[u] for u in units}
        m_loc = {u: jnp.max(log_end[u], axis=1, keepdims=True) for u in units}
        m_in, a_old, a_loc = {}, {}, {}
        for h in heads:
            m = m_ref[h:h + 1, :]
            for ci in group:
                u = (ci, h)
                m_in[u] = m
                m = jnp.maximum(g_row[u] + m_in[u], m_loc[u])
                a_old[u] = jnp.exp2(g_row[u] + m_in[u] - m)[:, 0:1]
                a_loc[u] = jnp.exp2(m_loc[u] - m)[:, 0:1]
            m_ref[h:h + 1, :] = m
        log_inter = {u: c_col[u] + m_in[u] for u in units}
        m_t = {u: jnp.maximum(log_inter[u], row_max[u]) for u in units}
        s_b = {u: (qk[u] * jnp.exp2(log_d[u] - m_t[u])).astype(BF16) for u in units}
        w_inter = {u: jnp.exp2(log_inter[u] - m_t[u]) for u in units}
        intra = {u: _dot(s_b[u], v_aug[u]) for u in units}
        kw_b = {u: (k_t[u] * jnp.exp2(log_end[u] - m_loc[u])).astype(BF16) for u in units}
        local = {u: _dot(kw_b[u], v_aug[u]) for u in units}
        inter = {}
        for h in heads:
            cn = jnp.concatenate([ct_ref[h], nm_ref[h]], axis=1)
            for ci in group:
                u = (ci, h)
                inter[u] = _dot(q_b[u], cn.astype(BF16))
                cn = a_old[u] * cn + a_loc[u] * local[u]
            ct_ref[h] = cn[:, 0:D]
            nm_ref[h] = cn[:, D:2 * D]
        hh = {}
        for u in units:
            both = intra[u] + jnp.concatenate([w_inter[u], w_inter[u]], axis=1) * inter[u]
            den = jnp.maximum(jnp.abs(both[:, D:2 * D]), jnp.exp2(-m_t[u]))
            hh[u] = both[:, 0:D] / den
        mu = {u: jnp.mean(hh[u], axis=-1, keepdims=True) for u in units}
        xc = {u: hh[u] - mu[u] for u in units}
        var = {u: jnp.mean(xc[u] * xc[u], axis=-1, keepdims=True) for u in units}
        for ci, h in units:
            u = (ci, h)
            cols = slice(h * D, (h + 1) * D)
            ln = xc[u] * lax.rsqrt(var[u] + EPS) * lnw_ref[:, cols]
            o_gate = jax.nn.sigmoid(oz_ref[rows[ci], h * D:(h + 1) * D])
            z = oz_ref[rows[ci], ML_WIDTH + h * D:ML_WIDTH + (h + 1) * D]
            out = (ln * o_gate + skip_ref[:, cols] * mc_ref[rows[ci], cols]) * _silu(z)
            y_ref[rows[ci], cols] = out.astype(BF16)

    assert (tb // L) % ML_CHUNK_GROUP == 0
    for c0 in range(0, tb // L, ML_CHUNK_GROUP):
        chunk_group(c0)


def _hg_level_table():
    l = np.arange(CHUNK)[:, None]
    s = np.arange(CHUNK)[None, :]
    x = l ^ s
    msb = np.floor(np.log2(np.maximum(x, 1))).astype(np.int32)
    return np.where(s < l, msb, np.where(s == l, HG_LEVELS, HG_LEVELS + 1)).astype(np.int32)


def _replicated_row(ref, blk, r):
    return ref[blk, pl.ds(r, SUBLANES, stride=0), :]


def _hg_reference_rows(cum_ref, level):
    b = 1 << level
    sub = lax.broadcasted_iota(jnp.int32, (SUBLANES, LANES), 0)
    col_blocks = []
    for blk in range(cum_ref.shape[0]):
        def row8(r, blk=blk):
            return _replicated_row(cum_ref, blk, r)
        pieces = []
        if 2 * b >= SUBLANES:
            for i in range(CHUNK // (2 * b)):
                pieces.extend([row8(i * 2 * b + b - 1)] * (2 * b // SUBLANES))
        else:
            for grp in range(CHUNK // SUBLANES):
                base = grp * SUBLANES
                acc = row8(base + b - 1)
                for i in range(1, SUBLANES // (2 * b)):
                    acc = jnp.where(sub >= i * 2 * b, row8(base + i * 2 * b + b - 1), acc)
                pieces.append(acc)
        col_blocks.append(jnp.concatenate(pieces, axis=0))
    return jnp.concatenate(col_blocks, axis=1)


def _hg_kernel(hn_ref, w_ref, lb_ref, gnw_ref, lv_ref, y_ref, pj_ref, cum_ref, st_ref):
    tb = hn_ref.shape[0]
    L = CHUNK
    D = HG_HEAD_DIM
    W = HG_WIDTH

    @pl.when(pl.program_id(1) == 0)
    def _():
        st_ref[...] = jnp.zeros(st_ref.shape, F32)

    pj_ref[...] = _dot(hn_ref[...], w_ref[...])
    tri = _tril_ones_bf16(L)
    lb = lb_ref[...]
    lb_pos = lb > 0.0

    def chunk(c, carry):
        r0 = _chunk_start(c)
        rows = pl.ds(r0, L)
        fx = pj_ref[rows, W:2 * W]
        a = jnp.abs(fx)
        t = jnp.exp(-a)
        pos = fx >= 0.0
        one_t = 1.0 + t
        log_num = jnp.where(pos, jnp.log(1.0 + lb * t), jnp.where(lb_pos, jnp.log(t + lb), -a))
        log2_f = (log_num - jnp.log(one_t)) * LOG2E
        k = (1.0 - lb) * jnp.where(pos, t, 1.0) / one_t
        q = _silu(pj_ref[rows, 0:W])
        v_b = pj_ref[rows, 2 * W:3 * W].astype(BF16)
        cum = _cumsum_time(log2_f, tri)
        _store_col_blocks(cum_ref, 0, 0, cum)
        lv = lv_ref[...]
        q_b = q.astype(BF16)
        k_b = k.astype(BF16)
        attn = []
        for h in range(HG_HEADS):
            cols = slice(h * D, (h + 1) * D)
            attn.append(jnp.where(lv == HG_LEVELS, _dot_nt(q_b[:, cols], k_b[:, cols]), 0.0))
        for level in range(HG_LEVELS):
            e = jnp.exp2(-jnp.abs((cum - _hg_reference_rows(cum_ref, level)).astype(BF16)))
            qe = q_b * e
            ke = k_b * e
            for h in range(HG_HEADS):
                cols = slice(h * D, (h + 1) * D)
                attn[h] = jnp.where(lv == level, _dot_nt(qe[:, cols], ke[:, cols]), attn[h])
        cum_last = cum[L - 1:L, :]
        q_dec = (q * jnp.exp2(cum)).astype(BF16)
        k_end = k * jnp.exp2(cum_last - cum)
        dec_last = jnp.exp2(cum_last)
        gz = _silu(pj_ref[rows,3 * W:4 * W])
        for h in range(HG_HEADS):
            cols = slice(h * D, (h + 1) * D)
            st = st_ref[h]
            o = _dot(attn[h].astype(BF16), v_b[:, cols]) + _dot_nt(q_dec[:, cols], st.astype(BF16))
            v_t = pj_ref[rows,2 * W + h * D:2 * W + (h + 1) * D].T.astype(BF16)
            st_ref[h] = st * dec_last[:, cols] + _dot(v_t, k_end[:, cols].astype(BF16))
            on = o * lax.rsqrt(jnp.mean(o * o, axis=-1, keepdims=True) + EPS)
            y_ref[rows, cols] = (on * gnw_ref[:, cols] * gz[:, cols]).astype(BF16)
        return carry

    _for_chunks(tb // L, chunk)


def _merge_kernel(x_ref, hn_ref, wg_ref, ys_ref, ym_ref, yh_ref, wbs_ref, wbm_ref, wbh_ref, wo_ref,
                  nnw_ref, o_ref, *hn_next_ref, last_layer):
    x = x_ref[...]
    hn = hn_ref[...]
    d = x.shape[-1]
    merged = jax.nn.sigmoid(_dot(hn, wg_ref[:, 0:d])) * _dot(ys_ref[...], wbs_ref[...])
    merged = merged + jax.nn.sigmoid(_dot(hn, wg_ref[:, d:2 * d])) * _dot(ym_ref[...], wbm_ref[...])
    merged = merged + jax.nn.sigmoid(_dot(hn, wg_ref[:, 2 * d:3 * d])) * _dot(yh_ref[...], wbh_ref[...])
    out = x + _dot(merged.astype(BF16), wo_ref[...])
    if last_layer:
        o_ref[...] = _rmsnorm(out, nnw_ref[...])
    else:
        o_ref[...] = out
        hn_next_ref[0][...] = _rmsnorm(out, nnw_ref[...]).astype(BF16)


def _segment_cast_kernel(w_ref, o_ref, *, keep_rows):
    w = w_ref[0]
    if keep_rows is not None:
        row = lax.broadcasted_iota(jnp.int32, w.shape, 0)
        w = jnp.where(row < keep_rows, w, 0.0)
    o_ref[...] = w.T.astype(BF16)


def _const_spec(shape):
    nd = len(shape)
    return pl.BlockSpec(shape, lambda b, t: (0,) * nd)


def _tok_spec(tb, width):
    return pl.BlockSpec((None, tb, width), lambda b, t: (b, t, 0))


def _layer_spec(block_shape, layer, col_block=0):
    nd = len(block_shape)
    return pl.BlockSpec((None,) + tuple(block_shape),
                        lambda b, t: (layer,) + (0,) * (nd - 1) + (col_block,))


def _with_specs(consts):
    pairs = [c if isinstance(c, tuple) else (c, _const_spec(c.shape)) for c in consts]
    return [a for a, _ in pairs], [sp for _, sp in pairs]


def _mixer_call(kernel_fn, name, x, consts, out_width, scratch_shapes, tb, extra_out_width=None):
    bsz, seq, d = x.shape
    consts, const_specs = _with_specs(consts)
    widths = [out_width] + ([] if extra_out_width is None else [extra_out_width])
    return pl.pallas_call(
        kernel_fn,
        grid=(bsz, seq // tb),
        in_specs=[_tok_spec(tb, d)] + const_specs,
        out_specs=[_tok_spec(tb, w) for w in widths],
        out_shape=[jax.ShapeDtypeStruct((bsz, seq, w), BF16) for w in widths],
        scratch_shapes=scratch_shapes,
        compiler_params=pltpu.CompilerParams(
            dimension_semantics=("parallel", "arbitrary"),
            vmem_limit_bytes=VMEM_LIMIT_BYTES),
        name=name,
    )(x, *consts)


def _row(v):
    return v.reshape(1, -1).astype(F32)


SEGMENT_ROWS = 1536


def _segment_cast(w_t, src, width, keep_rows=None):
    depth, _, d = w_t.shape
    rb = max(r for r in range(LANES, min(SEGMENT_ROWS, width) + 1, LANES) if width % r == 0)
    return pl.pallas_call(
        functools.partial(_segment_cast_kernel, keep_rows=keep_rows),
        grid=(depth, width // rb),
        in_specs=[pl.BlockSpec((pl.Element(1), pl.Element(rb), pl.Element(d)),
                               lambda l, j: (l, pl.multiple_of(src + j * rb, 16), 0))],
        out_specs=pl.BlockSpec((None, d, rb), lambda l, j: (l, 0, j)),
        out_shape=jax.ShapeDtypeStruct((depth, d, width), BF16),
        compiler_params=pltpu.CompilerParams(
            dimension_semantics=("parallel", "parallel"), vmem_limit_bytes=VMEM_LIMIT_BYTES),
        name='w_in_segment_cast',
    )(w_t)


def _split_w_in(w_in):
    w_t = jnp.swapaxes(w_in, 1, 2)
    o_dt = SSD_CONV_DIM
    o_z = o_dt + SSD_HEADS
    o_ml = o_z + SSD_WIDTH
    o_hg = o_ml + 3 * ML_WIDTH
    o_gate = o_hg + 4 * HG_WIDTH
    return dict(ssd=_segment_cast(w_t, 0, SSD_CONV_DIM),
                ssd_dt=_segment_cast(w_t, o_dt, LANES, keep_rows=SSD_HEADS),
                ssd_z=_segment_cast(w_t, o_z, SSD_WIDTH),
                ml=_segment_cast(w_t, o_ml, 3 * ML_WIDTH),
                hg=_segment_cast(w_t, o_hg, 4 * HG_WIDTH),
                gate=_segment_cast(w_t, o_gate, 3 * w_in.shape[1]))


def _layer(x, hn, layer, prm, last_layer):
    bsz, seq, d = x.shape
    tb = min(TOKEN_BLOCK, seq)
    assert seq % tb == 0 and tb % CHUNK == 0

    def lw(*names):
        return [(prm[k], _layer_spec(prm[k].shape[1:], layer)) for k in names]

    ssd_consts = lw('ssd', 'ssd_dt', 'ssd_z', 'ssd_conv_w', 'ssd_conv_b', 'ssd_dt_bias', 'ssd_a_log',
                    'ssd_d', 'ssd_norm_w')
    ssd_scratch = [pltpu.VMEM((SSD_CONV_DIM // LANES, tb + SUBLANES, LANES), F32),
                   pltpu.VMEM((tb, SSD_CONV_DIM), F32),
                   pltpu.VMEM((tb, LANES), F32), pltpu.VMEM((tb, SSD_WIDTH), F32),
                   pltpu.VMEM((SSD_GROUPS, SSD_STATE, SSD_WIDTH // SSD_GROUPS), F32)]
    if hn is None:
        y_ssd, hn = _mixer_call(functools.partial(_ssd_kernel, prenorm=True), 'ssd_mixer', x,
                                lw('norm_w') + ssd_consts, SSD_WIDTH, ssd_scratch, tb, extra_out_width=d)
    else:
        y_ssd, = _mixer_call(functools.partial(_ssd_kernel, prenorm=False), 'ssd_mixer', hn,
                             ssd_consts, SSD_WIDTH, ssd_scratch, tb)

    ml_consts = lw('ml', 'ml_conv_w', 'ml_conv_b', 'ml_wqk', 'ml_wv', 'ml_w_if', 'ml_b_if',
                   'ml_norm_w', 'ml_skip')
    y_ml, = _mixer_call(
        _ml_kernel, 'mlstm_mixer', hn, ml_consts, ML_WIDTH,
        [pltpu.VMEM((ML_WIDTH // LANES, tb + SUBLANES, LANES), F32), pltpu.VMEM((tb, ML_WIDTH), F32),
         pltpu.VMEM((tb, 2 * ML_WIDTH), F32), pltpu.VMEM((tb, 3 * ML_WIDTH), F32),
         pltpu.VMEM((tb, LANES), F32),
         pltpu.VMEM((ML_HEADS, ML_HEAD_DIM, ML_HEAD_DIM), F32),
         pltpu.VMEM((ML_HEADS, ML_HEAD_DIM, ML_HEAD_DIM), F32), pltpu.VMEM((SUBLANES, LANES), F32)], tb)

    hg_consts = lw('hg', 'hg_lb', 'hg_norm_w') + [jnp.asarray(_hg_level_table())]
    y_hg, = _mixer_call(
        _hg_kernel, 'hgrn2_mixer', hn, hg_consts, HG_WIDTH,
        [pltpu.VMEM((tb, 4 * HG_WIDTH), F32), pltpu.VMEM((HG_WIDTH // LANES, CHUNK, LANES), F32),
         pltpu.VMEM((HG_HEADS, HG_HEAD_DIM, HG_HEAD_DIM), F32)], tb)

    tb = min(MERGE_BLOCK, seq)
    next_norm = (prm['final_norm_w'] if last_layer
                 else (prm['norm_w'], _layer_spec(prm['norm_w'].shape[1:], layer + 1)))
    merge_consts_a, merge_specs_a = _with_specs(lw('gate'))
    merge_consts_b, merge_specs_b = _with_specs(
        lw('w_branch_ssd', 'w_branch_ml', 'w_branch_hg', 'w_out') + [next_norm])
    out_specs = [_tok_spec(tb, d)]
    out_shape = [jax.ShapeDtypeStruct((bsz, seq, d), F32)]
    if not last_layer:
        out_specs.append(_tok_spec(tb, d))
        out_shape.append(jax.ShapeDtypeStruct((bsz, seq, d), BF16))
    outs = pl.pallas_call(
        functools.partial(_merge_kernel, last_layer=last_layer),
        grid=(bsz, seq // tb),
        in_specs=([_tok_spec(tb, d), _tok_spec(tb, d)] + merge_specs_a
                  + [_tok_spec(tb, SSD_WIDTH), _tok_spec(tb, ML_WIDTH), _tok_spec(tb, HG_WIDTH)]
                  + merge_specs_b),
        out_specs=out_specs,
        out_shape=out_shape,
        compiler_params=pltpu.CompilerParams(
            dimension_semantics=("parallel", "parallel"),
            vmem_limit_bytes=VMEM_LIMIT_BYTES),
        name='merge_out',
    )(x, hn, *merge_consts_a, y_ssd, y_ml, y_hg, *merge_consts_b)
    return (outs[0], None) if last_layer else (outs[0], outs[1])


def _rows(v):
    return v.reshape(v.shape[0], 1, -1).astype(F32)


def kernel(x, norm_w, w_in, ssd_conv_w, ssd_conv_b, ssd_dt_bias, ssd_a_log, ssd_d, ssd_norm_w, ml_conv_w, ml_conv_b, ml_wq, ml_wk, ml_wv, ml_w_if, ml_b_if, ml_norm_w, ml_skip, hg_lower_bounds, hg_norm_w, w_branch_ssd, w_branch_ml, w_branch_hg, w_out, final_norm_w):
    depth = norm_w.shape[0]
    pad_h = ((0, 0), (0, LANES - SSD_HEADS))
    pad_if = LANES - 2 * ML_HEADS
    lbs = jnp.cumsum(jax.nn.softmax(hg_lower_bounds.astype(F32), axis=0), axis=0)
    lbs = lbs - lbs[0]
    prm = dict(
        **_split_w_in(w_in),
        w_branch_ssd=w_branch_ssd.astype(BF16), w_branch_ml=w_branch_ml.astype(BF16),
        w_branch_hg=w_branch_hg.astype(BF16), w_out=w_out.astype(BF16),
        norm_w=_rows(norm_w), final_norm_w=_row(final_norm_w),
        ssd_conv_w=ssd_conv_w.astype(F32), ssd_conv_b=_rows(ssd_conv_b),
        ssd_dt_bias=_rows(jnp.pad(ssd_dt_bias, pad_h)), ssd_a_log=_rows(jnp.pad(ssd_a_log, pad_h)),
        ssd_d=_rows(jnp.repeat(ssd_d, SSD_HEAD_DIM, axis=1)), ssd_norm_w=_rows(ssd_norm_w),
        ml_conv_w=ml_conv_w.astype(F32), ml_conv_b=_rows(ml_conv_b),
        ml_wqk=jnp.concatenate([ml_wq, ml_wk], axis=-1).astype(BF16),
        ml_wv=ml_wv.astype(BF16),
        ml_w_if=jnp.pad(ml_w_if, ((0, 0), (0, 0), (0, pad_if))).astype(BF16),
        ml_b_if=_rows(jnp.pad(ml_b_if, ((0, 0), (0, pad_if)))),
        ml_norm_w=_rows(ml_norm_w), ml_skip=_rows(ml_skip),
        hg_lb=_rows(lbs), hg_norm_w=_rows(hg_norm_w))
    hn = None
    for l in range(depth):
        x, hn = _layer(x, hn, l, prm, last_layer=(l == depth - 1))
    return x
```

```python
import functools

import numpy as np
import jax
import jax.numpy as jnp
from jax import lax
from jax.experimental import pallas as pl
from jax.experimental.pallas import tpu as pltpu

F32 = jnp.float32
BF16 = jnp.bfloat16

EPS = 1e-6
LOG2E = 1.4426950408889634
CONV_K = 4
LANES = 128
SUBLANES = 8
CHUNK = 128
TOKEN_BLOCK = 1024
MERGE_BLOCK = 1024
PROJ_ROWS = 256
ML_PROJ_ROWS = 512
ML_CHUNK_GROUP = 8
VMEM_LIMIT_BYTES = 56 * 1024 * 1024

SSD_HEAD_DIM = 64
SSD_HEADS = 16
SSD_GROUPS = 2
SSD_STATE = 128
SSD_WIDTH = SSD_HEADS * SSD_HEAD_DIM
SSD_CONV_DIM = SSD_WIDTH + 2 * SSD_GROUPS * SSD_STATE
ML_HEADS = 4
ML_HEAD_DIM = 128
ML_WIDTH = ML_HEADS * ML_HEAD_DIM
HG_HEADS = 4
HG_HEAD_DIM = 128
HG_WIDTH = HG_HEADS * HG_HEAD_DIM
HG_LEVELS = 7


def _dot(a, b):
    return jnp.dot(a, b, preferred_element_type=F32)


def _dot_nt(a, b):
    return lax.dot_general(a, b, (((1,), (1,)), ((), ())), preferred_element_type=F32)


def _rmsnorm(x, w):
    return x * lax.rsqrt(jnp.mean(x * x, axis=-1, keepdims=True) + EPS) * w


def _softplus(x):
    return jnp.maximum(x, 0.0) + jnp.log1p(jnp.exp(-jnp.abs(x)))


def _log_sigmoid(x):
    return jnp.minimum(x, 0.0) - jnp.log1p(jnp.exp(-jnp.abs(x)))


def _silu(x):
    return x * jax.nn.sigmoid(x)


def _tril_ones_bf16(n):
    r = lax.broadcasted_iota(jnp.int32, (n, n), 0)
    c = lax.broadcasted_iota(jnp.int32, (n, n), 1)
    return jnp.where(c <= r, 1.0, 0.0).astype(BF16)


def _cumsum_time(x, tri):
    hi = x.astype(BF16)
    r1 = x - hi.astype(F32)
    mid = r1.astype(BF16)
    lo = (r1 - mid.astype(F32)).astype(BF16)
    return _dot(tri, hi) + _dot(tri, mid) + _dot(tri, lo)


def _colb(x, j, n=LANES):
    return jnp.broadcast_to(x[:, j:j + 1], (x.shape[0], n))


def _rowb(x, j, m):
    return jnp.broadcast_to(x[j:j + 1, :], (m, x.shape[1]))


def _expand_heads(v):
    rows = v.shape[0]
    lane = lax.broadcasted_iota(jnp.int32, (rows, LANES), 1)
    parts = []
    for j in range(SSD_HEADS // 2):
        a = _colb(v, 2 * j)
        b = _colb(v, 2 * j + 1)
        parts.append(jnp.where(lane < SSD_HEAD_DIM, a, b))
    return jnp.concatenate(parts, axis=1)


def _chunk_start(c):
    return c * CHUNK if isinstance(c, int) else pl.multiple_of(c * CHUNK, CHUNK)


def _for_chunks(n, body):
    for c in range(n):
        body(c, 0)


def _store_col_blocks(dst_ref, first_block, row0, val):
    for i in range(val.shape[1] // LANES):
        dst_ref[first_block + i, pl.ds(row0, val.shape[0]), :] = val[:, i * LANES:(i + 1) * LANES]


def _causal_conv_silu(pre_ref, cw_ref, cb_ref, out_ref, r0, n, blocks=None):
    for blk in (range(pre_ref.shape[0]) if blocks is None else blocks):
        cols = slice(blk * LANES, (blk + 1) * LANES)
        acc = cb_ref[:, cols] + cw_ref[0:1, cols] * pre_ref[blk, pl.ds(r0 + SUBLANES - 3, n), :]
        for j in range(1, CONV_K):
            acc = acc + cw_ref[j:j + 1, cols] * pre_ref[blk, pl.ds(r0 + SUBLANES - 3 + j, n), :]
        out_ref[pl.ds(r0, n), cols] = _silu(acc)


def _conv_carry(pre_ref, tb):
    for blk in range(pre_ref.shape[0]):
        pre_ref[blk, 0:SUBLANES, :] = pre_ref[blk, pl.ds(tb, SUBLANES), :]


def _ssd_kernel(*refs, prenorm):
    refs = list(refs)
    src_ref = refs.pop(0)
    nw_ref = refs.pop(0) if prenorm else None
    w_ref, wdt_ref, wz_ref, cw_ref, cb_ref, dtb_ref, alog_ref, dskip_ref, gnw_ref, y_ref = refs[:10]
    del refs[:10]
    hn_out_ref = refs.pop(0) if prenorm else None
    pre_ref, xc_ref, dt_ref, z_ref, st_ref = refs
    tb = src_ref.shape[0]
    L = CHUNK

    @pl.when(pl.program_id(1) == 0)
    def _():
        pre_ref[:, 0:SUBLANES, :] = jnp.zeros((pre_ref.shape[0], SUBLANES, LANES), F32)
        st_ref[...] = jnp.zeros(st_ref.shape, F32)

    group = 4 * LANES
    for r0 in range(0, tb, PROJ_ROWS):
        prow = pl.ds(r0, PROJ_ROWS)
        if prenorm:
            hn = _rmsnorm(src_ref[prow, :], nw_ref[...]).astype(BF16)
            hn_out_ref[prow, :] = hn
        else:
            hn = src_ref[prow, :]
        conv_after = []
        for c0 in range(0, SSD_CONV_DIM, group):
            _store_col_blocks(pre_ref, c0 // LANES, SUBLANES + r0, _dot(hn, w_ref[:, c0:c0 + group]))
            for blocks in conv_after:
                _causal_conv_silu(pre_ref, cw_ref, cb_ref, xc_ref, r0, PROJ_ROWS, blocks)
            conv_after = [range(c0 // LANES, (c0 + group) // LANES)]
        dt_ref[prow, :] = _dot(hn, wdt_ref[...])
        half = SSD_WIDTH // 2
        z_ref[prow, 0:half] = _dot(hn, wz_ref[:, 0:half])
        _causal_conv_silu(pre_ref, cw_ref, cb_ref, xc_ref, r0, PROJ_ROWS, conv_after[0])
        z_ref[prow, half:] = _dot(hn, wz_ref[:, half:])
    _conv_carry(pre_ref, tb)

    lane_row = lax.broadcasted_iota(jnp.int32, (1, LANES), 1)
    a_row = jnp.where(lane_row < SSD_HEADS, -jnp.exp(alog_ref[...]) * LOG2E, 0.0)
    r_i = lax.broadcasted_iota(jnp.int32, (L, L), 0)
    c_i = lax.broadcasted_iota(jnp.int32, (L, L), 1)
    causal = c_i <= r_i
    lane = lax.broadcasted_iota(jnp.int32, (L, LANES), 1)
    lo_half = lane < SSD_HEAD_DIM
    tri = _tril_ones_bf16(L)
    gs = SSD_WIDTH // SSD_GROUPS
    pairs_per_group = SSD_HEADS // SSD_GROUPS // 2

    def chunk(c, carry):
        r0 = _chunk_start(c)
        rows = pl.ds(r0, L)
        xs = xc_ref[rows, 0:SSD_WIDTH]
        dt = _softplus(dt_ref[rows, :] + dtb_ref[...])
        cum = _cumsum_time(dt * a_row, tri)
        cum_last = cum[L - 1:L, :]
        r_t = (cum - jnp.log(dt) * LOG2E).T
        dtw_t = (dt * jnp.exp2(cum_last - cum)).T
        dec_last_x = _expand_heads(jnp.exp2(cum_last))
        xs_b = xs.astype(BF16)

        y_parts = []
        for g in range(SSD_GROUPS):
            bm = xc_ref[rows, SSD_WIDTH + g * SSD_STATE:SSD_WIDTH + (g + 1) * SSD_STATE]
            cm = xc_ref[rows, SSD_WIDTH + (SSD_GROUPS + g) * SSD_STATE:
                        SSD_WIDTH + (SSD_GROUPS + g + 1) * SSD_STATE]
            cm_b = cm.astype(BF16)
            cb = _dot_nt(cm_b, bm.astype(BF16))
            bm_t = bm.T
            h_t = st_ref[g]
            y_off = _dot(cm_b, h_t.astype(BF16))
            new_state = []
            for jp in range(pairs_per_group):
                pair = g * pairs_per_group + jp
                lanes = slice(pair * LANES, (pair + 1) * LANES)
                sc, bsc, dec = [], [], []
                for h in (2 * pair, 2 * pair + 1):
                    c_col = _colb(cum, h)
                    seg = c_col - _rowb(r_t, h, L)
                    sc.append((cb * jnp.exp2(jnp.where(causal, seg, -jnp.inf))).astype(BF16))
                    bsc.append((bm_t * _rowb(dtw_t, h, L)).astype(BF16))
                    dec.append(jnp.exp2(c_col))
                xp = xs_b[:, lanes]
                zero = jnp.zeros_like(xp)
                rhs = jnp.concatenate([jnp.where(lo_half, xp, zero), jnp.where(lo_half, zero, xp)], axis=0)
                y_diag = _dot(jnp.concatenate(sc, axis=1), rhs)
                local = _dot(jnp.concatenate(bsc, axis=1), rhs)
                y_parts.append(y_diag + y_off[:, jp * LANES:(jp + 1) * LANES]
                               * jnp.where(lo_half, dec[0], dec[1]))
                new_state.append(h_t[:, jp * LANES:(jp + 1) * LANES] * dec_last_x[:, lanes] + local)
            st_ref[g] = jnp.concatenate(new_state, axis=1)
        y = jnp.concatenate(y_parts, axis=1) + xs * dskip_ref[...]
        y = y * _silu(z_ref[rows, :])
        outs = []
        for g in range(SSD_GROUPS):
            yg = y[:, g * gs:(g + 1) * gs]
            outs.append(yg * lax.rsqrt(jnp.mean(yg * yg, axis=-1, keepdims=True) + EPS))
        y_ref[rows, :] = (jnp.concatenate(outs, axis=1) * gnw_ref[...]).astype(BF16)
        return carry

    _for_chunks(tb // L, chunk)


def _ml_kernel(hn_ref, w_ref, cw_ref, cb_ref, wqk_ref, wv_ref, wif_ref, bif_ref, lnw_ref,
               skip_ref, y_ref, pre_ref, mc_ref, oz_ref, qkv_ref, if_ref, ct_ref, nm_ref, m_ref):
    tb = hn_ref.shape[0]
    L = CHUNK
    D = ML_HEAD_DIM

    @pl.when(pl.program_id(1) == 0)
    def _():
        pre_ref[:, 0:SUBLANES, :] = jnp.zeros((pre_ref.shape[0], SUBLANES, LANES), F32)
        ct_ref[...] = jnp.zeros(ct_ref.shape, F32)
        nm_ref[...] = jnp.zeros(nm_ref.shape, F32)
        m_ref[...] = jnp.zeros(m_ref.shape, F32)

    for r0 in range(0, tb, ML_PROJ_ROWS):
        prow = pl.ds(r0, ML_PROJ_ROWS)
        hn = hn_ref[prow, :]
        _store_col_blocks(pre_ref, 0, SUBLANES + r0, _dot(hn, w_ref[:, 0:ML_WIDTH]))
        oz_ref[prow, :] = _dot(hn, w_ref[:, ML_WIDTH:])
        for h in range(ML_HEADS):
            v = _dot(pre_ref[h, pl.ds(SUBLANES + r0, ML_PROJ_ROWS), :].astype(BF16), wv_ref[h])
            qkv_ref[prow, h * 3 * D + 2 * D:(h + 1) * 3 * D] = v
        _causal_conv_silu(pre_ref, cw_ref, cb_ref, mc_ref, r0, ML_PROJ_ROWS)
        for h in range(ML_HEADS):
            qkv_ref[prow, h * 3 * D:h * 3 * D + 2 * D] = _dot(
                mc_ref[prow, h * D:(h + 1) * D].astype(BF16), wqk_ref[h])
        if_ref[prow, :] = _dot(qkv_ref[prow, :].astype(BF16), wif_ref[...]) + bif_ref[...]
    _conv_carry(pre_ref, tb)

    r_i = lax.broadcasted_iota(jnp.int32, (L, L), 0)
    c_i = lax.broadcasted_iota(jnp.int32, (L, L), 1)
    causal = c_i <= r_i
    tri = _tril_ones_bf16(L)
    scale = D ** -0.5

    ones_b = jnp.ones((L, D), BF16)
    heads = range(ML_HEADS)
    group = range(ML_CHUNK_GROUP)
    units = [(ci, h) for ci in group for h in heads]

    def chunk_group(c0):
        rows = [pl.ds((c0 + ci) * L, L) for ci in group]
        if_pre = [if_ref[rows[ci], :] for ci in group]
        cum = [_cumsum_time(_log_sigmoid(if_pre[ci]) * LOG2E, tri) for ci in group]
        cum_t = [cum[ci].T for ci in group]
        if_t = [(if_pre[ci] * LOG2E).T for ci in group]
        q_b = {(ci, h): (qkv_ref[rows[ci], h * 3 * D:h * 3 * D + D] * scale).astype(BF16) for ci, h in units}
        k = {(ci, h): qkv_ref[rows[ci], h * 3 * D + D:h * 3 * D + 2 * D] for ci, h in units}
        v_aug = {(ci, h): jnp.concatenate(
            [qkv_ref[rows[ci], h * 3 * D + 2 * D:(h + 1) * 3 * D].astype(BF16), ones_b], axis=1)
            for ci, h in units}
        qk = {u: _dot_nt(q_b[u], k[u].astype(BF16)) for u in units}
        k_t = {u: k[u].T for u in units}
        c_col = {(ci, h): _colb(cum[ci], ML_HEADS + h) for ci, h in units}
        a_row = {(ci, h): if_t[ci][h:h + 1, :] - cum_t[ci][ML_HEADS + h:ML_HEADS + h + 1, :]
                 for ci, h in units}
        log_d = {u: jnp.where(causal, c_col[u] + a_row[u], -jnp.inf) for u in units}
        row_max = {u: jnp.max(log_d[u], axis=1, keepdims=True) for u in units}
        g_row = {u: c_col[u][L - 1:L, :] for u in units}
        log_end = {u: g_row[u] + a_row[u] for u in units}
        m_loc = {u: jnp.max(log_end[u], axis=1, keepdims=True) for u in units}
        m_in, a_old, a_loc = {}, {}, {}
        for h in heads:
            m = m_ref[h:h + 1, :]
            for ci in group:
                u = (ci, h)
                m_in[u] = m
                m = jnp.maximum(g_row[u] + m_in[u], m_loc[u])
                a_old[u] = jnp.exp2(g_row[u] + m_in[u] - m)[:, 0:1]
                a_loc[u] = jnp.exp2(m_loc[u] - m)[:, 0:1]
            m_ref[h:h + 1, :] = m
        log_inter = {u: c_col[u] + m_in[u] for u in units}
        m_t = {u: jnp.maximum(log_inter[u], row_max[u]) for u in units}
        s_b = {u: (qk[u] * jnp.exp2(log_d[u] - m_t[u])).astype(BF16) for u in units}
        w_inter = {u: jnp.exp2(log_inter[u] - m_t[u]) for u in units}
        intra = {u: _dot(s_b[u], v_aug[u]) for u in units}
        kw_b = {u: (k_t[u] * jnp.exp2(log_end[u] - m_loc[u])).astype(BF16) for u in units}
        local = {u: _dot(kw_b[u], v_aug[u]) for u in units}
        inter = {}
        for h in heads:
            cn = jnp.concatenate([ct_ref[h], nm_ref[h]], axis=1)
            for ci in group:
                u = (ci, h)
                inter[u] = _dot(q_b[u], cn.astype(BF16))
                cn = a_old[u] * cn + a_loc[u] * local[u]
            ct_ref[h] = cn[:, 0:D]
            nm_ref[h] = cn[:, D:2 * D]
        hh = {}
        for u in units:
            both = intra[u] + jnp.concatenate([w_inter[u], w_inter[u]], axis=1) * inter[u]
            den = jnp.maximum(jnp.abs(both[:, D:2 * D]), jnp.exp2(-m_t[u]))
            hh[u] = both[:, 0:D] / den
        mu = {u: jnp.mean(hh[u], axis=-1, keepdims=True) for u in units}
        xc = {u: hh[u] - mu[u] for u in units}
        var = {u: jnp.mean(xc[u] * xc[u], axis=-1, keepdims=True) for u in units}
        for ci, h in units:
            u = (ci, h)
            cols = slice(h * D, (h + 1) * D)
            ln = xc[u] * lax.rsqrt(var[u] + EPS) * lnw_ref[:, cols]
            o_gate = jax.nn.sigmoid(oz_ref[rows[ci], h * D:(h + 1) * D])
            z = oz_ref[rows[ci], ML_WIDTH + h * D:ML_WIDTH + (h + 1) * D]
            out = (ln * o_gate + skip_ref[:, cols] * mc_ref[rows[ci], cols]) * _silu(z)
            y_ref[rows[ci], cols] = out.astype(BF16)

    assert (tb // L) % ML_CHUNK_GROUP == 0
    for c0 in range(0, tb // L, ML_CHUNK_GROUP):
        chunk_group(c0)


def _hg_level_table():
    l = np.arange(CHUNK)[:, None]
    s = np.arange(CHUNK)[None, :]
    x = l ^ s
    msb = np.floor(np.log2(np.maximum(x, 1))).astype(np.int32)
    return np.where(s < l, msb, np.where(s == l, HG_LEVELS, HG_LEVELS + 1)).astype(np.int32)


def _replicated_row(ref, blk, r):
    return ref[blk, pl.ds(r, SUBLANES, stride=0), :]


def _hg_reference_rows(cum_ref, level):
    b = 1 << level
    sub = lax.broadcasted_iota(jnp.int32, (SUBLANES, LANES), 0)
    col_blocks = []
    for blk in range(cum_ref.shape[0]):
        def row8(r, blk=blk):
            return _replicated_row(cum_ref, blk, r)
        pieces = []
        if 2 * b >= SUBLANES:
            for i in range(CHUNK // (2 * b)):
                pieces.extend([row8(i * 2 * b + b - 1)] * (2 * b // SUBLANES))
        else:
            for grp in range(CHUNK // SUBLANES):
                base = grp * SUBLANES
                acc = row8(base + b - 1)
                for i in range(1, SUBLANES // (2 * b)):
                    acc = jnp.where(sub >= i * 2 * b, row8(base + i * 2 * b + b - 1), acc)
                pieces.append(acc)
        col_blocks.append(jnp.concatenate(pieces, axis=0))
    return jnp.concatenate(col_blocks, axis=1)


def _hg_kernel(hn_ref, w_ref, lb_ref, gnw_ref, lv_ref, y_ref, pj_ref, cum_ref, st_ref):
    tb = hn_ref.shape[0]
    L = CHUNK
    D = HG_HEAD_DIM
    W = HG_WIDTH

    @pl.when(pl.program_id(1) == 0)
    def _():
        st_ref[...] = jnp.zeros(st_ref.shape, F32)

    pj_ref[...] = _dot(hn_ref[...], w_ref[...])
    tri = _tril_ones_bf16(L)
    lb = lb_ref[...]
    lb_pos = lb > 0.0

    def chunk(c, carry):
        r0 = _chunk_start(c)
        rows = pl.ds(r0, L)
        fx = pj_ref[rows, W:2 * W]
        a = jnp.abs(fx)
        t = jnp.exp(-a)
        pos = fx >= 0.0
        one_t = 1.0 + t
        log_num = jnp.where(pos, jnp.log(1.0 + lb * t), jnp.where(lb_pos, jnp.log(t + lb), -a))
        log2_f = (log_num - jnp.log(one_t)) * LOG2E
        k = (1.0 - lb) * jnp.where(pos, t, 1.0) / one_t
        q = _silu(pj_ref[rows, 0:W])
        v_b = pj_ref[rows, 2 * W:3 * W].astype(BF16)
        cum = _cumsum_time(log2_f, tri)
        _store_col_blocks(cum_ref, 0, 0, cum)
        lv = lv_ref[...]
        q_b = q.astype(BF16)
        k_b = k.astype(BF16)
        attn = []
        for h in range(HG_HEADS):
            cols = slice(h * D, (h + 1) * D)
            attn.append(jnp.where(lv == HG_LEVELS, _dot_nt(q_b[:, cols], k_b[:, cols]), 0.0))
        for level in range(HG_LEVELS):
            e = jnp.exp2(-jnp.abs((cum - _hg_reference_rows(cum_ref, level)).astype(BF16)))
            qe = q_b * e
            ke = k_b * e
            for h in range(HG_HEADS):
                cols = slice(h * D, (h + 1) * D)
                attn[h] = jnp.where(lv == level, _dot_nt(qe[:, cols], ke[:, cols]), attn[h])
        cum_last = cum[L - 1:L, :]
        q_dec = (q * jnp.exp2(cum)).astype(BF16)
        k_end = k * jnp.exp2(cum_last - cum)
        dec_last = jnp.exp2(cum_last)
        gz = _silu(pj_ref[rows,3 * W:4 * W])
        for h in range(HG_HEADS):
            cols = slice(h * D, (h + 1) * D)
            st = st_ref[h]
            o = _dot(attn[h].astype(BF16), v_b[:, cols]) + _dot_nt(q_dec[:, cols], st.astype(BF16))
            v_t = pj_ref[rows,2 * W + h * D:2 * W + (h + 1) * D].T.astype(BF16)
            st_ref[h] = st * dec_last[:, cols] + _dot(v_t, k_end[:, cols].astype(BF16))
            on = o * lax.rsqrt(jnp.mean(o * o, axis=-1, keepdims=True) + EPS)
            y_ref[rows, cols] = (on * gnw_ref[:, cols] * gz[:, cols]).astype(BF16)
        return carry

    _for_chunks(tb // L, chunk)


def _merge_kernel(x_ref, hn_ref, wg_ref, ys_ref, ym_ref, yh_ref, wbs_ref, wbm_ref, wbh_ref, wo_ref,
                  nnw_ref, o_ref, *hn_next_ref, last_layer):
    x = x_ref[...]
    hn = hn_ref[...]
    d = x.shape[-1]
    merged = jax.nn.sigmoid(_dot(hn, wg_ref[:, 0:d])) * _dot(ys_ref[...], wbs_ref[...])
    merged = merged + jax.nn.sigmoid(_dot(hn, wg_ref[:, d:2 * d])) * _dot(ym_ref[...], wbm_ref[...])
    merged = merged + jax.nn.sigmoid(_dot(hn, wg_ref[:, 2 * d:3 * d])) * _dot(yh_ref[...], wbh_ref[...])
    out = x + _dot(merged.astype(BF16), wo_ref[...])
    if last_layer:
        o_ref[...] = _rmsnorm(out, nnw_ref[...])
    else:
        o_ref[...] = out
        hn_next_ref[0][...] = _rmsnorm(out, nnw_ref[...]).astype(BF16)


def _segment_cast_kernel(w_ref, o_ref, *, keep_rows):
    w = w_ref[0]
    if keep_rows is not None:
        row = lax.broadcasted_iota(jnp.int32, w.shape, 0)
        w = jnp.where(row < keep_rows, w, 0.0)
    o_ref[...] = w.T.astype(BF16)


def _const_spec(shape):
    nd = len(shape)
    return pl.BlockSpec(shape, lambda b, t: (0,) * nd)


def _tok_spec(tb, width):
    return pl.BlockSpec((None, tb, width), lambda b, t: (b, t, 0))


def _layer_spec(block_shape, layer, col_block=0):
    nd = len(block_shape)
    return pl.BlockSpec((None,) + tuple(block_shape),
                        lambda b, t: (layer,) + (0,) * (nd - 1) + (col_block,))


def _with_specs(consts):
    pairs = [c if isinstance(c, tuple) else (c, _const_spec(c.shape)) for c in consts]
    return [a for a, _ in pairs], [sp for _, sp in pairs]


def _mixer_call(kernel_fn, name, x, consts, out_width, scratch_shapes, tb, extra_out_width=None):
    bsz, seq, d = x.shape
    consts, const_specs = _with_specs(consts)
    widths = [out_width] + ([] if extra_out_width is None else [extra_out_width])
    return pl.pallas_call(
        kernel_fn,
        grid=(bsz, seq // tb),
        in_specs=[_tok_spec(tb, d)] + const_specs,
        out_specs=[_tok_spec(tb, w) for w in widths],
        out_shape=[jax.ShapeDtypeStruct((bsz, seq, w), BF16) for w in widths],
        scratch_shapes=scratch_shapes,
        compiler_params=pltpu.CompilerParams(
            dimension_semantics=("parallel", "arbitrary"),
            vmem_limit_bytes=VMEM_LIMIT_BYTES),
        name=name,
    )(x, *consts)


def _row(v):
    return v.reshape(1, -1).astype(F32)


SEGMENT_ROWS = 1536


def _segment_cast(w_t, src, width, keep_rows=None):
    depth, _, d = w_t.shape
    rb = max(r for r in range(LANES, min(SEGMENT_ROWS, width) + 1, LANES) if width % r == 0)
    return pl.pallas_call(
        functools.partial(_segment_cast_kernel, keep_rows=keep_rows),
        grid=(depth, width // rb),
        in_specs=[pl.BlockSpec((pl.Element(1), pl.Element(rb), pl.Element(d)),
                               lambda l, j: (l, pl.multiple_of(src + j * rb, 16), 0))],
        out_specs=pl.BlockSpec((None, d, rb), lambda l, j: (l, 0, j)),
        out_shape=jax.ShapeDtypeStruct((depth, d, width), BF16),
        compiler_params=pltpu.CompilerParams(
            dimension_semantics=("parallel", "parallel"), vmem_limit_bytes=VMEM_LIMIT_BYTES),
        name='w_in_segment_cast',
    )(w_t)


def _split_w_in(w_in):
    w_t = jnp.swapaxes(w_in, 1, 2)
    o_dt = SSD_CONV_DIM
    o_z = o_dt + SSD_HEADS
    o_ml = o_z + SSD_WIDTH
    o_hg = o_ml + 3 * ML_WIDTH
    o_gate = o_hg + 4 * HG_WIDTH
    return dict(ssd=_segment_cast(w_t, 0, SSD_CONV_DIM),
                ssd_dt=_segment_cast(w_t, o_dt, LANES, keep_rows=SSD_HEADS),
                ssd_z=_segment_cast(w_t, o_z, SSD_WIDTH),
                ml=_segment_cast(w_t, o_ml, 3 * ML_WIDTH),
                hg=_segment_cast(w_t, o_hg, 4 * HG_WIDTH),
                gate=_segment_cast(w_t, o_gate, 3 * w_in.shape[1]))


def _layer(x, hn, layer, prm, last_layer):
    bsz, seq, d = x.shape
    tb = min(TOKEN_BLOCK, seq)
    assert seq % tb == 0 and tb % CHUNK == 0

    def lw(*names):
        return [(prm[k], _layer_spec(prm[k].shape[1:], layer)) for k in names]

    ssd_consts = lw('ssd', 'ssd_dt', 'ssd_z', 'ssd_conv_w', 'ssd_conv_b', 'ssd_dt_bias', 'ssd_a_log',
                    'ssd_d', 'ssd_norm_w')
    ssd_scratch = [pltpu.VMEM((SSD_CONV_DIM // LANES, tb + SUBLANES, LANES), F32),
                   pltpu.VMEM((tb, SSD_CONV_DIM), F32),
                   pltpu.VMEM((tb, LANES), F32), pltpu.VMEM((tb, SSD_WIDTH), F32),
                   pltpu.VMEM((SSD_GROUPS, SSD_STATE, SSD_WIDTH // SSD_GROUPS), F32)]
    if hn is None:
        y_ssd, hn = _mixer_call(functools.partial(_ssd_kernel, prenorm=True), 'ssd_mixer', x,
                                lw('norm_w') + ssd_consts, SSD_WIDTH, ssd_scratch, tb, extra_out_width=d)
    else:
        y_ssd, = _mixer_call(functools.partial(_ssd_kernel, prenorm=False), 'ssd_mixer', hn,
                             ssd_consts, SSD_WIDTH, ssd_scratch, tb)

    ml_consts = lw('ml', 'ml_conv_w', 'ml_conv_b', 'ml_wqk', 'ml_wv', 'ml_w_if', 'ml_b_if',
                   'ml_norm_w', 'ml_skip')
    y_ml, = _mixer_call(
        _ml_kernel, 'mlstm_mixer', hn, ml_consts, ML_WIDTH,
        [pltpu.VMEM((ML_WIDTH // LANES, tb + SUBLANES, LANES), F32), pltpu.VMEM((tb, ML_WIDTH), F32),
         pltpu.VMEM((tb, 2 * ML_WIDTH), F32), pltpu.VMEM((tb, 3 * ML_WIDTH), F32),
         pltpu.VMEM((tb, LANES), F32),
         pltpu.VMEM((ML_HEADS, ML_HEAD_DIM, ML_HEAD_DIM), F32),
         pltpu.VMEM((ML_HEADS, ML_HEAD_DIM, ML_HEAD_DIM), F32), pltpu.VMEM((SUBLANES, LANES), F32)], tb)

    hg_consts = lw('hg', 'hg_lb', 'hg_norm_w') + [jnp.asarray(_hg_level_table())]
    y_hg, = _mixer_call(
        _hg_kernel, 'hgrn2_mixer', hn, hg_consts, HG_WIDTH,
        [pltpu.VMEM((tb, 4 * HG_WIDTH), F32), pltpu.VMEM((HG_WIDTH // LANES, CHUNK, LANES), F32),
         pltpu.VMEM((HG_HEADS, HG_HEAD_DIM, HG_HEAD_DIM), F32)], tb)

    tb = min(MERGE_BLOCK, seq)
    next_norm = (prm['final_norm_w'] if last_layer
                 else (prm['norm_w'], _layer_spec(prm['norm_w'].shape[1:], layer + 1)))
    merge_consts_a, merge_specs_a = _with_specs(lw('gate'))
    merge_consts_b, merge_specs_b = _with_specs(
        lw('w_branch_ssd', 'w_branch_ml', 'w_branch_hg', 'w_out') + [next_norm])
    out_specs = [_tok_spec(tb, d)]
    out_shape = [jax.ShapeDtypeStruct((bsz, seq, d), F32)]
    if not last_layer:
        out_specs.append(_tok_spec(tb, d))
        out_shape.append(jax.ShapeDtypeStruct((bsz, seq, d), BF16))
    outs = pl.pallas_call(
        functools.partial(_merge_kernel, last_layer=last_layer),
        grid=(bsz, seq // tb),
        in_specs=([_tok_spec(tb, d), _tok_spec(tb, d)] + merge_specs_a
                  + [_tok_spec(tb, SSD_WIDTH), _tok_spec(tb, ML_WIDTH), _tok_spec(tb, HG_WIDTH)]
                  + merge_specs_b),
        out_specs=out_specs,
        out_shape=out_shape,
        compiler_params=pltpu.CompilerParams(
            dimension_semantics=("parallel", "parallel"),
            vmem_limit_bytes=VMEM_LIMIT_BYTES),
        name='merge_out',
    )(x, hn, *merge_consts_a, y_ssd, y_ml, y_hg, *merge_consts_b)
    return (outs[0], None) if last_layer else (outs[0], outs[1])


def _rows(v):
    return v.reshape(v.shape[0], 1, -1).astype(F32)


def kernel(x, norm_w, w_in, ssd_conv_w, ssd_conv_b, ssd_dt_bias, ssd_a_log, ssd_d, ssd_norm_w, ml_conv_w, ml_conv_b, ml_wq, ml_wk, ml_wv, ml_w_if, ml_b_if, ml_norm_w, ml_skip, hg_lower_bounds, hg_norm_w, w_branch_ssd, w_branch_ml, w_branch_hg, w_out, final_norm_w):
    depth = norm_w.shape[0]
    pad_h = ((0, 0), (0, LANES - SSD_HEADS))
    pad_if = LANES - 2 * ML_HEADS
    lbs = jnp.cumsum(jax.nn.softmax(hg_lower_bounds.astype(F32), axis=0), axis=0)
    lbs = lbs - lbs[0]
    prm = dict(
        **_split_w_in(w_in),
        w_branch_ssd=w_branch_ssd.astype(BF16), w_branch_ml=w_branch_ml.astype(BF16),
        w_branch_hg=w_branch_hg.astype(BF16), w_out=w_out.astype(BF16),
        norm_w=_rows(norm_w), final_norm_w=_row(final_norm_w),
        ssd_conv_w=ssd_conv_w.astype(F32), ssd_conv_b=_rows(ssd_conv_b),
        ssd_dt_bias=_rows(jnp.pad(ssd_dt_bias, pad_h)), ssd_a_log=_rows(jnp.pad(ssd_a_log, pad_h)),
        ssd_d=_rows(jnp.repeat(ssd_d, SSD_HEAD_DIM, axis=1)), ssd_norm_w=_rows(ssd_norm_w),
        ml_conv_w=ml_conv_w.astype(F32), ml_conv_b=_rows(ml_conv_b),
        ml_wqk=jnp.concatenate([ml_wq, ml_wk], axis=-1).astype(BF16),
        ml_wv=ml_wv.astype(BF16),
        ml_w_if=jnp.pad(ml_w_if, ((0, 0), (0, 0), (0, pad_if))).astype(BF16),
        ml_b_if=_rows(jnp.pad(ml_b_if, ((0, 0), (0, pad_if)))),
        ml_norm_w=_rows(ml_norm_w), ml_skip=_rows(ml_skip),
        hg_lb=_rows(lbs), hg_norm_w=_rows(hg_norm_w))
    hn = None
    for l in range(depth):
        x, hn = _layer(x, hn, l, prm, last_layer=(l == depth - 1))
    return x
```

```python
import functools

import numpy as np
import jax
import jax.numpy as jnp
from jax import lax
from jax.experimental import pallas as pl
from jax.experimental.pallas import tpu as pltpu

F32 = jnp.float32
BF16 = jnp.bfloat16

EPS = 1e-6
LOG2E = 1.4426950408889634
CONV_K = 4
LANES = 128
SUBLANES = 8
CHUNK = 128
TOKEN_BLOCK = 1024
MERGE_BLOCK = 1024
PROJ_ROWS = 256
ML_PROJ_ROWS = 512
ML_CHUNK_GROUP = 8
VMEM_LIMIT_BYTES = 56 * 1024 * 1024

SSD_HEAD_DIM = 64
SSD_HEADS = 16
SSD_GROUPS = 2
SSD_STATE = 128
SSD_WIDTH = SSD_HEADS * SSD_HEAD_DIM
SSD_CONV_DIM = SSD_WIDTH + 2 * SSD_GROUPS * SSD_STATE
ML_HEADS = 4
ML_HEAD_DIM = 128
ML_WIDTH = ML_HEADS * ML_HEAD_DIM
HG_HEADS = 4
HG_HEAD_DIM = 128
HG_WIDTH = HG_HEADS * HG_HEAD_DIM
HG_LEVELS = 7


def _dot(a, b):
    return jnp.dot(a, b, preferred_element_type=F32)


def _dot_nt(a, b):
    return lax.dot_general(a, b, (((1,), (1,)), ((), ())), preferred_element_type=F32)


def _rmsnorm(x, w):
    return x * lax.rsqrt(jnp.mean(x * x, axis=-1, keepdims=True) + EPS) * w


def _softplus(x):
    return jnp.maximum(x, 0.0) + jnp.log1p(jnp.exp(-jnp.abs(x)))


def _log_sigmoid(x):
    return jnp.minimum(x, 0.0) - jnp.log1p(jnp.exp(-jnp.abs(x)))


def _silu(x):
    return x * jax.nn.sigmoid(x)


def _tril_ones_bf16(n):
    r = lax.broadcasted_iota(jnp.int32, (n, n), 0)
    c = lax.broadcasted_iota(jnp.int32, (n, n), 1)
    return jnp.where(c <= r, 1.0, 0.0).astype(BF16)


def _cumsum_time(x, tri):
    hi = x.astype(BF16)
    r1 = x - hi.astype(F32)
    mid = r1.astype(BF16)
    lo = (r1 - mid.astype(F32)).astype(BF16)
    return _dot(tri, hi) + _dot(tri, mid) + _dot(tri, lo)


def _colb(x, j, n=LANES):
    return jnp.broadcast_to(x[:, j:j + 1], (x.shape[0], n))


def _rowb(x, j, m):
    return jnp.broadcast_to(x[j:j + 1, :], (m, x.shape[1]))


def _expand_heads(v):
    rows = v.shape[0]
    lane = lax.broadcasted_iota(jnp.int32, (rows, LANES), 1)
    parts = []
    for j in range(SSD_HEADS // 2):
        a = _colb(v, 2 * j)
        b = _colb(v, 2 * j + 1)
        parts.append(jnp.where(lane < SSD_HEAD_DIM, a, b))
    return jnp.concatenate(parts, axis=1)


def _chunk_start(c):
    return c * CHUNK if isinstance(c, int) else pl.multiple_of(c * CHUNK, CHUNK)


def _for_chunks(n, body):
    for c in range(n):
        body(c, 0)


def _store_col_blocks(dst_ref, first_block, row0, val):
    for i in range(val.shape[1] // LANES):
        dst_ref[first_block + i, pl.ds(row0, val.shape[0]), :] = val[:, i * LANES:(i + 1) * LANES]


def _causal_conv_silu(pre_ref, cw_ref, cb_ref, out_ref, r0, n, blocks=None):
    for blk in (range(pre_ref.shape[0]) if blocks is None else blocks):
        cols = slice(blk * LANES, (blk + 1) * LANES)
        acc = cb_ref[:, cols] + cw_ref[0:1, cols] * pre_ref[blk, pl.ds(r0 + SUBLANES - 3, n), :]
        for j in range(1, CONV_K):
            acc = acc + cw_ref[j:j + 1, cols] * pre_ref[blk, pl.ds(r0 + SUBLANES - 3 + j, n), :]
        out_ref[pl.ds(r0, n), cols] = _silu(acc)


def _conv_carry(pre_ref, tb):
    for blk in range(pre_ref.shape[0]):
        pre_ref[blk, 0:SUBLANES, :] = pre_ref[blk, pl.ds(tb, SUBLANES), :]


def _ssd_kernel(*refs, prenorm):
    refs = list(refs)
    src_ref = refs.pop(0)
    nw_ref = refs.pop(0) if prenorm else None
    w_ref, wdt_ref, wz_ref, cw_ref, cb_ref, dtb_ref, alog_ref, dskip_ref, gnw_ref, y_ref = refs[:10]
    del refs[:10]
    hn_out_ref = refs.pop(0) if prenorm else None
    pre_ref, xc_ref, dt_ref, z_ref, st_ref = refs
    tb = src_ref.shape[0]
    L = CHUNK

    @pl.when(pl.program_id(1) == 0)
    def _():
        pre_ref[:, 0:SUBLANES, :] = jnp.zeros((pre_ref.shape[0], SUBLANES, LANES), F32)
        st_ref[...] = jnp.zeros(st_ref.shape, F32)

    group = 4 * LANES
    for r0 in range(0, tb, PROJ_ROWS):
        prow = pl.ds(r0, PROJ_ROWS)
        if prenorm:
            hn = _rmsnorm(src_ref[prow, :], nw_ref[...]).astype(BF16)
            hn_out_ref[prow, :] = hn
        else:
            hn = src_ref[prow, :]
        conv_after = []
        for c0 in range(0, SSD_CONV_DIM, group):
            _store_col_blocks(pre_ref, c0 // LANES, SUBLANES + r0, _dot(hn, w_ref[:, c0:c0 + group]))
            for blocks in conv_after:
                _causal_conv_silu(pre_ref, cw_ref, cb_ref, xc_ref, r0, PROJ_ROWS, blocks)
            conv_after = [range(c0 // LANES, (c0 + group) // LANES)]
        dt_ref[prow, :] = _dot(hn, wdt_ref[...])
        half = SSD_WIDTH // 2
        z_ref[prow, 0:half] = _dot(hn, wz_ref[:, 0:half])
        _causal_conv_silu(pre_ref, cw_ref, cb_ref, xc_ref, r0, PROJ_ROWS, conv_after[0])
        z_ref[prow, half:] = _dot(hn, wz_ref[:, half:])
    _conv_carry(pre_ref, tb)

    lane_row = lax.broadcasted_iota(jnp.int32, (1, LANES), 1)
    a_row = jnp.where(lane_row < SSD_HEADS, -jnp.exp(alog_ref[...]) * LOG2E, 0.0)
    r_i = lax.broadcasted_iota(jnp.int32, (L, L), 0)
    c_i = lax.broadcasted_iota(jnp.int32, (L, L), 1)
    causal = c_i <= r_i
    lane = lax.broadcasted_iota(jnp.int32, (L, LANES), 1)
    lo_half = lane < SSD_HEAD_DIM
    tri = _tril_ones_bf16(L)
    gs = SSD_WIDTH // SSD_GROUPS
    pairs_per_group = SSD_HEADS // SSD_GROUPS // 2

    def chunk(c, carry):
        r0 = _chunk_start(c)
        rows = pl.ds(r0, L)
        dt = _softplus(dt_ref[rows, :] + dtb_ref[...])
        cum = _cumsum_time(dt * a_row, tri)
        cum_last = cum[L - 1:L, :]
        r_t = (cum - jnp.log(dt) * LOG2E).T
        dtw_t = (dt * jnp.exp2(cum_last - cum)).T
        dec_last_x = _expand_heads(jnp.exp2(cum_last))

        y_parts = []
        for g in range(SSD_GROUPS):
            bm = xc_ref[rows, SSD_WIDTH + g * SSD_STATE:SSD_WIDTH + (g + 1) * SSD_STATE]
            cm = xc_ref[rows, SSD_WIDTH + (SSD_GROUPS + g) * SSD_STATE:
                        SSD_WIDTH + (SSD_GROUPS + g + 1) * SSD_STATE]
            cm_b = cm.astype(BF16)
            cb = _dot_nt(cm_b, bm.astype(BF16))
            bm_t = bm.T
            h_t = st_ref[g]
            y_off = _dot(cm_b, h_t.astype(BF16))
            new_state = []
            for jp in range(pairs_per_group):
                pair = g * pairs_per_group + jp
                lanes = slice(pair * LANES, (pair + 1) * LANES)
                sc, bsc, dec = [], [], []
                for h in (2 * pair, 2 * pair + 1):
                    c_col = _colb(cum, h)
                    seg = c_col - _rowb(r_t, h, L)
                    sc.append((cb * jnp.exp2(jnp.where(causal, seg, -jnp.inf))).astype(BF16))
                    bsc.append((bm_t * _rowb(dtw_t, h, L)).astype(BF16))
                    dec.append(jnp.exp2(c_col))
                xs_p = xc_ref[rows, lanes]
                xp = xs_p.astype(BF16)
                zero = jnp.zeros_like(xp)
                rhs = jnp.concatenate([jnp.where(lo_half, xp, zero), jnp.where(lo_half, zero, xp)], axis=0)
                y_diag = _dot(jnp.concatenate(sc, axis=1), rhs)
                local = _dot(jnp.concatenate(bsc, axis=1), rhs)
                y_pair = (y_diag + y_off[:, jp * LANES:(jp + 1) * LANES]
                          * jnp.where(lo_half, dec[0], dec[1]))
                y_parts.append((y_pair + xs_p * dskip_ref[:, lanes]) * _silu(z_ref[rows, lanes]))
                new_state.append(h_t[:, jp * LANES:(jp + 1) * LANES] * dec_last_x[:, lanes] + local)
            st_ref[g] = jnp.concatenate(new_state, axis=1)
        y = jnp.concatenate(y_parts, axis=1)
        outs = []
        for g in range(SSD_GROUPS):
            yg = y[:, g * gs:(g + 1) * gs]
            outs.append(yg * lax.rsqrt(jnp.mean(yg * yg, axis=-1, keepdims=True) + EPS))
        y_ref[rows, :] = (jnp.concatenate(outs, axis=1) * gnw_ref[...]).astype(BF16)
        return carry

    _for_chunks(tb // L, chunk)


def _ml_kernel(hn_ref, w_ref, cw_ref, cb_ref, wqk_ref, wv_ref, wif_ref, bif_ref, lnw_ref,
               skip_ref, y_ref, pre_ref, mc_ref, oz_ref, qkv_ref, if_ref, ct_ref, nm_ref, m_ref):
    tb = hn_ref.shape[0]
    L = CHUNK
    D = ML_HEAD_DIM

    @pl.when(pl.program_id(1) == 0)
    def _():
        pre_ref[:, 0:SUBLANES, :] = jnp.zeros((pre_ref.shape[0], SUBLANES, LANES), F32)
        ct_ref[...] = jnp.zeros(ct_ref.shape, F32)
        nm_ref[...] = jnp.zeros(nm_ref.shape, F32)
        m_ref[...] = jnp.zeros(m_ref.shape, F32)

    for r0 in range(0, tb, ML_PROJ_ROWS):
        prow = pl.ds(r0, ML_PROJ_ROWS)
        hn = hn_ref[prow, :]
        _store_col_blocks(pre_ref, 0, SUBLANES + r0, _dot(hn, w_ref[:, 0:ML_WIDTH]))
        oz_ref[prow, :] = _dot(hn, w_ref[:, ML_WIDTH:])
        for h in range(ML_HEADS):
            v = _dot(pre_ref[h, pl.ds(SUBLANES + r0, ML_PROJ_ROWS), :].astype(BF16), wv_ref[h])
            qkv_ref[prow, h * 3 * D + 2 * D:(h + 1) * 3 * D] = v
        _causal_conv_silu(pre_ref, cw_ref, cb_ref, mc_ref, r0, ML_PROJ_ROWS)
        for h in range(ML_HEADS):
            qkv_ref[prow, h * 3 * D:h * 3 * D + 2 * D] = _dot(
                mc_ref[prow, h * D:(h + 1) * D].astype(BF16), wqk_ref[h])
        if_ref[prow, :] = _dot(qkv_ref[prow, :].astype(BF16), wif_ref[...]) + bif_ref[...]
    _conv_carry(pre_ref, tb)

    r_i = lax.broadcasted_iota(jnp.int32, (L, L), 0)
    c_i = lax.broadcasted_iota(jnp.int32, (L, L), 1)
    causal = c_i <= r_i
    tri = _tril_ones_bf16(L)
    scale = D ** -0.5

    ones_b = jnp.ones((L, D), BF16)
    heads = range(ML_HEADS)
    group = range(ML_CHUNK_GROUP)
    units = [(ci, h) for ci in group for h in heads]

    def chunk_group(c0):
        rows = [pl.ds((c0 + ci) * L, L) for ci in group]
        if_pre = [if_ref[rows[ci], :] for ci in group]
        cum = [_cumsum_time(_log_sigmoid(if_pre[ci]) * LOG2E, tri) for ci in group]
        cum_t = [cum[ci].T for ci in group]
        if_t = [(if_pre[ci] * LOG2E).T for ci in group]
        q_b = {(ci, h): (qkv_ref[rows[ci], h * 3 * D:h * 3 * D + D] * scale).astype(BF16) for ci, h in units}
        k = {(ci, h): qkv_ref[rows[ci], h * 3 * D + D:h * 3 * D + 2 * D] for ci, h in units}
        v_aug = {(ci, h): jnp.concatenate(
            [qkv_ref[rows[ci], h * 3 * D + 2 * D:(h + 1) * 3 * D].astype(BF16), ones_b], axis=1)
            for ci, h in units}
        qk = {u: _dot_nt(q_b[u], k[u].astype(BF16)) for u in units}
        k_t = {u: k[u].T for u in units}
        c_col = {(ci, h): _colb(cum[ci], ML_HEADS + h) for ci, h in units}
        a_row = {(ci, h): if_t[ci][h:h + 1, :] - cum_t[ci][ML_HEADS + h:ML_HEADS + h + 1, :]
                 for ci, h in units}
        log_d = {u: jnp.where(causal, c_col[u] + a_row[u], -jnp.inf) for u in units}
        row_max = {u: jnp.max(log_d[u], axis=1, keepdims=True) for u in units}
        g_row = {u: c_col[u][L - 1:L, :] for u in units}
        log_end = {u: g_row[u] + a_row[u] for u in units}
        m_loc = {u: jnp.max(log_end[u], axis=1, keepdims=True) for u in units}
        m_in, a_old, a_loc = {}, {}, {}
        for h in heads:
            m = m_ref[h:h + 1, :]
            for ci in group:
                u = (ci, h)
                m_in[u] = m
                m = jnp.maximum(g_row[u] + m_in[u], m_loc[u])
                a_old[u] = jnp.exp2(g_row[u] + m_in[u] - m)[:, 0:1]
                a_loc[u] = jnp.exp2(m_loc[u] - m)[:, 0:1]
            m_ref[h:h + 1, :] = m
        log_inter = {u: c_col[u] + m_in[u] for u in units}
        m_t = {u: jnp.maximum(log_inter[u], row_max[u]) for u in units}
        s_b = {u: (qk[u] * jnp.exp2(log_d[u] - m_t[u])).astype(BF16) for u in units}
        w_inter = {u: jnp.exp2(log_inter[u] - m_t[u]) for u in units}
        intra = {u: _dot(s_b[u], v_aug[u]) for u in units}
        kw_b = {u: (k_t[u] * jnp.exp2(log_end[u] - m_loc[u])).astype(BF16) for u in units}
        local = {u: _dot(kw_b[u], v_aug[u]) for u in units}
        inter = {}
        for h in heads:
            cn = jnp.concatenate([ct_ref[h], nm_ref[h]], axis=1)
            for ci in group:
                u = (ci, h)
                inter[u] = _dot(q_b[u], cn.astype(BF16))
                cn = a_old[u] * cn + a_loc[u] * local[u]
            ct_ref[h] = cn[:, 0:D]
            nm_ref[h] = cn[:, D:2 * D]
        hh = {}
        for u in units:
            both = intra[u] + jnp.concatenate([w_inter[u], w_inter[u]], axis=1) * inter[u]
            den = jnp.maximum(jnp.abs(both[:, D:2 * D]), jnp.exp2(-m_t[u]))
            hh[u] = both[:, 0:D] / den
        mu = {u: jnp.mean(hh[u], axis=-1, keepdims=True) for u in units}
        xc = {u: hh[u] - mu[u] for u in units}
        var = {u: jnp.mean(xc[u] * xc[u], axis=-1, keepdims=True) for u in units}
        for ci, h in units:
            u = (ci, h)
            cols = slice(h * D, (h + 1) * D)
            ln = xc[u] * lax.rsqrt(var[u] + EPS) * lnw_ref[:, cols]
            o_gate = jax.nn.sigmoid(oz_ref[rows[ci], h * D:(h + 1) * D])
            z = oz_ref[rows[ci], ML_WIDTH + h * D:ML_WIDTH + (h + 1) * D]
            out = (ln * o_gate + skip_ref[:, cols] * mc_ref[rows[ci], cols]) * _silu(z)
            y_ref[rows[ci], cols] = out.astype(BF16)

    assert (tb // L) % ML_CHUNK_GROUP == 0
    for c0 in range(0, tb // L, ML_CHUNK_GROUP):
        chunk_group(c0)


def _hg_level_table():
    l = np.arange(CHUNK)[:, None]
    s = np.arange(CHUNK)[None, :]
    x = l ^ s
    msb = np.floor(np.log2(np.maximum(x, 1))).astype(np.int32)
    return np.where(s < l, msb, np.where(s == l, HG_LEVELS, HG_LEVELS + 1)).astype(np.int32)


def _replicated_row(ref, blk, r):
    return ref[blk, pl.ds(r, SUBLANES, stride=0), :]


def _hg_reference_rows(cum_ref, level):
    b = 1 << level
    sub = lax.broadcasted_iota(jnp.int32, (SUBLANES, LANES), 0)
    col_blocks = []
    for blk in range(cum_ref.shape[0]):
        def row8(r, blk=blk):
            return _replicated_row(cum_ref, blk, r)
        pieces = []
        if 2 * b >= SUBLANES:
            for i in range(CHUNK // (2 * b)):
                pieces.extend([row8(i * 2 * b + b - 1)] * (2 * b // SUBLANES))
        else:
            for grp in range(CHUNK // SUBLANES):
                base = grp * SUBLANES
                acc = row8(base + b - 1)
                for i in range(1, SUBLANES // (2 * b)):
                    acc = jnp.where(sub >= i * 2 * b, row8(base + i * 2 * b + b - 1), acc)
                pieces.append(acc)
        col_blocks.append(jnp.concatenate(pieces, axis=0))
    return jnp.concatenate(col_blocks, axis=1)


def _hg_kernel(hn_ref, w_ref, lb_ref, gnw_ref, lv_ref, y_ref, pj_ref, cum_ref, st_ref):
    tb = hn_ref.shape[0]
    L = CHUNK
    D = HG_HEAD_DIM
    W = HG_WIDTH

    @pl.when(pl.program_id(1) == 0)
    def _():
        st_ref[...] = jnp.zeros(st_ref.shape, F32)

    pj_ref[...] = _dot(hn_ref[...], w_ref[...])
    tri = _tril_ones_bf16(L)
    lb = lb_ref[...]
    lb_pos = lb > 0.0

    def chunk(c, carry):
        r0 = _chunk_start(c)
        rows = pl.ds(r0, L)
        fx = pj_ref[rows, W:2 * W]
        a = jnp.abs(fx)
        t = jnp.exp(-a)
        pos = fx >= 0.0
        one_t = 1.0 + t
        log_num = jnp.where(pos, jnp.log(1.0 + lb * t), jnp.where(lb_pos, jnp.log(t + lb), -a))
        log2_f = (log_num - jnp.log(one_t)) * LOG2E
        k = (1.0 - lb) * jnp.where(pos, t, 1.0) / one_t
        q = _silu(pj_ref[rows, 0:W])
        v_b = pj_ref[rows, 2 * W:3 * W].astype(BF16)
        cum = _cumsum_time(log2_f, tri)
        _store_col_blocks(cum_ref, 0, 0, cum)
        lv = lv_ref[...]
        q_b = q.astype(BF16)
        k_b = k.astype(BF16)
        attn = []
        for h in range(HG_HEADS):
            cols = slice(h * D, (h + 1) * D)
            attn.append(jnp.where(lv == HG_LEVELS, _dot_nt(q_b[:, cols], k_b[:, cols]), 0.0))
        for level in range(HG_LEVELS):
            e = jnp.exp2(-jnp.abs((cum - _hg_reference_rows(cum_ref, level)).astype(BF16)))
            qe = q_b * e
            ke = k_b * e
            for h in range(HG_HEADS):
                cols = slice(h * D, (h + 1) * D)
                attn[h] = jnp.where(lv == level, _dot_nt(qe[:, cols], ke[:, cols]), attn[h])
        cum_last = cum[L - 1:L, :]
        q_dec = (q * jnp.exp2(cum)).astype(BF16)
        k_end = k * jnp.exp2(cum_last - cum)
        dec_last = jnp.exp2(cum_last)
        gz = _silu(pj_ref[rows,3 * W:4 * W])
        for h in range(HG_HEADS):
            cols = slice(h * D, (h + 1) * D)
            st = st_ref[h]
            o = _dot(attn[h].astype(BF16), v_b[:, cols]) + _dot_nt(q_dec[:, cols], st.astype(BF16))
            v_t = pj_ref[rows,2 * W + h * D:2 * W + (h + 1) * D].T.astype(BF16)
            st_ref[h] = st * dec_last[:, cols] + _dot(v_t, k_end[:, cols].astype(BF16))
            on = o * lax.rsqrt(jnp.mean(o * o, axis=-1, keepdims=True) + EPS)
            y_ref[rows, cols] = (on * gnw_ref[:, cols] * gz[:, cols]).astype(BF16)
        return carry

    _for_chunks(tb // L, chunk)


def _merge_kernel(x_ref, hn_ref, wg_ref, ys_ref, ym_ref, yh_ref, wbs_ref, wbm_ref, wbh_ref, wo_ref,
                  nnw_ref, o_ref, *hn_next_ref, last_layer):
    x = x_ref[...]
    hn = hn_ref[...]
    d = x.shape[-1]
    merged = jax.nn.sigmoid(_dot(hn, wg_ref[:, 0:d])) * _dot(ys_ref[...], wbs_ref[...])
    merged = merged + jax.nn.sigmoid(_dot(hn, wg_ref[:, d:2 * d])) * _dot(ym_ref[...], wbm_ref[...])
    merged = merged + jax.nn.sigmoid(_dot(hn, wg_ref[:, 2 * d:3 * d])) * _dot(yh_ref[...], wbh_ref[...])
    out = x + _dot(merged.astype(BF16), wo_ref[...])
    if last_layer:
        o_ref[...] = _rmsnorm(out, nnw_ref[...])
    else:
        o_ref[...] = out
        hn_next_ref[0][...] = _rmsnorm(out, nnw_ref[...]).astype(BF16)


def _segment_cast_kernel(w_ref, o_ref, *, keep_rows):
    w = w_ref[0]
    if keep_rows is not None:
        row = lax.broadcasted_iota(jnp.int32, w.shape, 0)
        w = jnp.where(row < keep_rows, w, 0.0)
    o_ref[...] = w.T.astype(BF16)


def _const_spec(shape):
    nd = len(shape)
    return pl.BlockSpec(shape, lambda b, t: (0,) * nd)


def _tok_spec(tb, width):
    return pl.BlockSpec((None, tb, width), lambda b, t: (b, t, 0))


def _layer_spec(block_shape, layer, col_block=0):
    nd = len(block_shape)
    return pl.BlockSpec((None,) + tuple(block_shape),
                        lambda b, t: (layer,) + (0,) * (nd - 1) + (col_block,))


def _with_specs(consts):
    pairs = [c if isinstance(c, tuple) else (c, _const_spec(c.shape)) for c in consts]
    return [a for a, _ in pairs], [sp for _, sp in pairs]


def _mixer_call(kernel_fn, name, x, consts, out_width, scratch_shapes, tb, extra_out_width=None):
    bsz, seq, d = x.shape
    consts, const_specs = _with_specs(consts)
    widths = [out_width] + ([] if extra_out_width is None else [extra_out_width])
    return pl.pallas_call(
        kernel_fn,
        grid=(bsz, seq // tb),
        in_specs=[_tok_spec(tb, d)] + const_specs,
        out_specs=[_tok_spec(tb, w) for w in widths],
        out_shape=[jax.ShapeDtypeStruct((bsz, seq, w), BF16) for w in widths],
        scratch_shapes=scratch_shapes,
        compiler_params=pltpu.CompilerParams(
            dimension_semantics=("parallel", "arbitrary"),
            vmem_limit_bytes=VMEM_LIMIT_BYTES),
        name=name,
    )(x, *consts)


def _row(v):
    return v.reshape(1, -1).astype(F32)


SEGMENT_ROWS = 1536


def _segment_cast(w_t, src, width, keep_rows=None):
    depth, _, d = w_t.shape
    rb = max(r for r in range(LANES, min(SEGMENT_ROWS, width) + 1, LANES) if width % r == 0)
    return pl.pallas_call(
        functools.partial(_segment_cast_kernel, keep_rows=keep_rows),
        grid=(depth, width // rb),
        in_specs=[pl.BlockSpec((pl.Element(1), pl.Element(rb), pl.Element(d)),
                               lambda l, j: (l, pl.multiple_of(src + j * rb, 16), 0))],
        out_specs=pl.BlockSpec((None, d, rb), lambda l, j: (l, 0, j)),
        out_shape=jax.ShapeDtypeStruct((depth, d, width), BF16),
        compiler_params=pltpu.CompilerParams(
            dimension_semantics=("parallel", "parallel"), vmem_limit_bytes=VMEM_LIMIT_BYTES),
        name='w_in_segment_cast',
    )(w_t)


def _split_w_in(w_in):
    w_t = jnp.swapaxes(w_in, 1, 2)
    o_dt = SSD_CONV_DIM
    o_z = o_dt + SSD_HEADS
    o_ml = o_z + SSD_WIDTH
    o_hg = o_ml + 3 * ML_WIDTH
    o_gate = o_hg + 4 * HG_WIDTH
    return dict(ssd=_segment_cast(w_t, 0, SSD_CONV_DIM),
                ssd_dt=_segment_cast(w_t, o_dt, LANES, keep_rows=SSD_HEADS),
                ssd_z=_segment_cast(w_t, o_z, SSD_WIDTH),
                ml=_segment_cast(w_t, o_ml, 3 * ML_WIDTH),
                hg=_segment_cast(w_t, o_hg, 4 * HG_WIDTH),
                gate=_segment_cast(w_t, o_gate, 3 * w_in.shape[1]))


def _layer(x, hn, layer, prm, last_layer):
    bsz, seq, d = x.shape
    tb = min(TOKEN_BLOCK, seq)
    assert seq % tb == 0 and tb % CHUNK == 0

    def lw(*names):
        return [(prm[k], _layer_spec(prm[k].shape[1:], layer)) for k in names]

    ssd_consts = lw('ssd', 'ssd_dt', 'ssd_z', 'ssd_conv_w', 'ssd_conv_b', 'ssd_dt_bias', 'ssd_a_log',
                    'ssd_d', 'ssd_norm_w')
    ssd_scratch = [pltpu.VMEM((SSD_CONV_DIM // LANES, tb + SUBLANES, LANES), F32),
                   pltpu.VMEM((tb, SSD_CONV_DIM), F32),
                   pltpu.VMEM((tb, LANES), F32), pltpu.VMEM((tb, SSD_WIDTH), F32),
                   pltpu.VMEM((SSD_GROUPS, SSD_STATE, SSD_WIDTH // SSD_GROUPS), F32)]
    if hn is None:
        y_ssd, hn = _mixer_call(functools.partial(_ssd_kernel, prenorm=True), 'ssd_mixer', x,
                                lw('norm_w') + ssd_consts, SSD_WIDTH, ssd_scratch, tb, extra_out_width=d)
    else:
        y_ssd, = _mixer_call(functools.partial(_ssd_kernel, prenorm=False), 'ssd_mixer', hn,
                             ssd_consts, SSD_WIDTH, ssd_scratch, tb)

    ml_consts = lw('ml', 'ml_conv_w', 'ml_conv_b', 'ml_wqk', 'ml_wv', 'ml_w_if', 'ml_b_if',
                   'ml_norm_w', 'ml_skip')
    y_ml, = _mixer_call(
        _ml_kernel, 'mlstm_mixer', hn, ml_consts, ML_WIDTH,
        [pltpu.VMEM((ML_WIDTH // LANES, tb + SUBLANES, LANES), F32), pltpu.VMEM((tb, ML_WIDTH), F32),
         pltpu.VMEM((tb, 2 * ML_WIDTH), F32), pltpu.VMEM((tb, 3 * ML_WIDTH), F32),
         pltpu.VMEM((tb, LANES), F32),
         pltpu.VMEM((ML_HEADS, ML_HEAD_DIM, ML_HEAD_DIM), F32),
         pltpu.VMEM((ML_HEADS, ML_HEAD_DIM, ML_HEAD_DIM), F32), pltpu.VMEM((SUBLANES, LANES), F32)], tb)

    hg_consts = lw('hg', 'hg_lb', 'hg_norm_w') + [jnp.asarray(_hg_level_table())]
    y_hg, = _mixer_call(
        _hg_kernel, 'hgrn2_mixer', hn, hg_consts, HG_WIDTH,
        [pltpu.VMEM((tb, 4 * HG_WIDTH), F32), pltpu.VMEM((HG_WIDTH // LANES, CHUNK, LANES), F32),
         pltpu.VMEM((HG_HEADS, HG_HEAD_DIM, HG_HEAD_DIM), F32)], tb)

    tb = min(MERGE_BLOCK, seq)
    next_norm = (prm['final_norm_w'] if last_layer
                 else (prm['norm_w'], _layer_spec(prm['norm_w'].shape[1:], layer + 1)))
    merge_consts_a, merge_specs_a = _with_specs(lw('gate'))
    merge_consts_b, merge_specs_b = _with_specs(
        lw('w_branch_ssd', 'w_branch_ml', 'w_branch_hg', 'w_out') + [next_norm])
    out_specs = [_tok_spec(tb, d)]
    out_shape = [jax.ShapeDtypeStruct((bsz, seq, d), F32)]
    if not last_layer:
        out_specs.append(_tok_spec(tb, d))
        out_shape.append(jax.ShapeDtypeStruct((bsz, seq, d), BF16))
    outs = pl.pallas_call(
        functools.partial(_merge_kernel, last_layer=last_layer),
        grid=(bsz, seq // tb),
        in_specs=([_tok_spec(tb, d), _tok_spec(tb, d)] + merge_specs_a
                  + [_tok_spec(tb, SSD_WIDTH), _tok_spec(tb, ML_WIDTH), _tok_spec(tb, HG_WIDTH)]
                  + merge_specs_b),
        out_specs=out_specs,
        out_shape=out_shape,
        compiler_params=pltpu.CompilerParams(
            dimension_semantics=("parallel", "parallel"),
            vmem_limit_bytes=VMEM_LIMIT_BYTES),
        name='merge_out',
    )(x, hn, *merge_consts_a, y_ssd, y_ml, y_hg, *merge_consts_b)
    return (outs[0], None) if last_layer else (outs[0], outs[1])


def _rows(v):
    return v.reshape(v.shape[0], 1, -1).astype(F32)


def kernel(x, norm_w, w_in, ssd_conv_w, ssd_conv_b, ssd_dt_bias, ssd_a_log, ssd_d, ssd_norm_w, ml_conv_w, ml_conv_b, ml_wq, ml_wk, ml_wv, ml_w_if, ml_b_if, ml_norm_w, ml_skip, hg_lower_bounds, hg_norm_w, w_branch_ssd, w_branch_ml, w_branch_hg, w_out, final_norm_w):
    depth = norm_w.shape[0]
    pad_h = ((0, 0), (0, LANES - SSD_HEADS))
    pad_if = LANES - 2 * ML_HEADS
    lbs = jnp.cumsum(jax.nn.softmax(hg_lower_bounds.astype(F32), axis=0), axis=0)
    lbs = lbs - lbs[0]
    prm = dict(
        **_split_w_in(w_in),
        w_branch_ssd=w_branch_ssd.astype(BF16), w_branch_ml=w_branch_ml.astype(BF16),
        w_branch_hg=w_branch_hg.astype(BF16), w_out=w_out.astype(BF16),
        norm_w=_rows(norm_w), final_norm_w=_row(final_norm_w),
        ssd_conv_w=ssd_conv_w.astype(F32), ssd_conv_b=_rows(ssd_conv_b),
        ssd_dt_bias=_rows(jnp.pad(ssd_dt_bias, pad_h)), ssd_a_log=_rows(jnp.pad(ssd_a_log, pad_h)),
        ssd_d=_rows(jnp.repeat(ssd_d, SSD_HEAD_DIM, axis=1)), ssd_norm_w=_rows(ssd_norm_w),
        ml_conv_w=ml_conv_w.astype(F32), ml_conv_b=_rows(ml_conv_b),
        ml_wqk=jnp.concatenate([ml_wq, ml_wk], axis=-1).astype(BF16),
        ml_wv=ml_wv.astype(BF16),
        ml_w_if=jnp.pad(ml_w_if, ((0, 0), (0, 0), (0, pad_if))).astype(BF16),
        ml_b_if=_rows(jnp.pad(ml_b_if, ((0, 0), (0, pad_if)))),
        ml_norm_w=_rows(ml_norm_w), ml_skip=_rows(ml_skip),
        hg_lb=_rows(lbs), hg_norm_w=_rows(hg_norm_w))
    hn = None
    for l in range(depth):
        x, hn = _layer(x, hn, l, prm, last_layer=(l == depth - 1))
    return x
```
